```python
import math
import jax
import jax.numpy as jnp
from jax import lax
import numpy as np

D_MODEL = 2048
BATCH = 4
SEQ = 4096
DEPTH = 4

PLE_DIM = 256
N_EVEN = (DEPTH + 1) // 2
N_ODD = DEPTH // 2
QBLK = 128
EPS = 1e-6
A_HEADS = 8
A_KV = 2
A_GRP = A_HEADS // A_KV
A_DH = 128
A_Q = A_HEADS * A_DH
A_KVW = A_KV * A_DH
CMP_LEN = 32
CMP_STRIDE = 16
CMP_HID = 256
SLC_BLK = 64
SLC_TOPK = 16
SLC_QBLK = 64
WIN_A = 512
CONV_DIM = 1024
CONV_K = 3
C_HEADS = 32
C_KV = 4
C_GRP = C_HEADS // C_KV
C_DH = 64
C_Q = C_HEADS * C_DH
C_KVW = C_KV * C_DH
WIN_C = 128
N_BUCKETS = 32
MAX_DIST = 1024
N_BIAS_HEADS = A_HEADS + C_HEADS
D_FF = 5632
N_EXPERTS = 8
TOP_K = 2
MOE_BLK = 512
IN_E = A_Q + 6 * A_KVW + 3 * A_HEADS + 3 * CONV_DIM
MIX_E = A_Q + CONV_DIM
IN_O = C_Q + 2 * C_KVW
MIX_O = C_Q

kernel_name = 'hybrid_nsa_shortconv_swa_moe_trunk'


def rmsnorm(x, g):
    xf = x.astype(jnp.float32)
    y = xf * lax.rsqrt(jnp.mean(xf * xf, axis=-1, keepdims=True) + EPS)
    return (y * g.astype(jnp.float32)).astype(x.dtype)


def t5_bucket(dist):
    n = jnp.maximum(jnp.asarray(dist, jnp.int32), 0)
    exact = N_BUCKETS // 2
    nf = jnp.maximum(n, 1).astype(jnp.float32)
    large = exact + (jnp.log(nf / exact) / math.log(MAX_DIST / exact) * (N_BUCKETS - exact)).astype(jnp.int32)
    return jnp.where(n < exact, n, jnp.minimum(large, N_BUCKETS - 1))


def masked_softmax(s, mask):
    s = jnp.where(mask, s, -jnp.inf)
    m = jnp.max(s, axis=-1, keepdims=True)
    m = jnp.where(jnp.isfinite(m), m, 0.0)
    e = jnp.where(mask, jnp.exp(s - m), 0.0)
    return e / jnp.maximum(jnp.sum(e, axis=-1, keepdims=True), 1e-30)


def band_blocks(k, n_prev):
    b_, g_, s_, dh = k.shape
    nb = s_ // QBLK
    kb = k.reshape(b_, g_, nb, QBLK, dh)
    kb = jnp.pad(kb, ((0, 0), (0, 0), (n_prev, 0), (0, 0), (0, 0)))
    return jnp.concatenate([kb[:, :, j:j + nb] for j in range(n_prev + 1)], axis=3)


def band_geometry(nb, n_prev, window):
    qi = np.arange(QBLK)[:, None] + n_prev * QBLK
    kj = np.arange((n_prev + 1) * QBLK)[None, :]
    dist = qi - kj
    valid = (dist >= 0) & (dist < window)
    key_pos = np.arange(nb)[:, None, None] * QBLK - n_prev * QBLK + kj[None]
    mask = valid[None] & (key_pos >= 0)
    return dist, jnp.asarray(mask)


def nsa_attention(q, kc, vc, ks, vs, kw, vw, g, cmp_pos, cmp_w1, cmp_w2, bias_tab):
    b_, s_, _ = q.shape
    f32 = jnp.float32
    qh = q.reshape(b_, s_, A_KV, A_GRP, A_DH).transpose(0, 2, 3, 1, 4) * (A_DH ** -0.5)

    def heads(t):
        return t.reshape(b_, s_, A_KV, A_DH).transpose(0, 2, 1, 3)

    kc, vc, ks, vs, kw, vw = heads(kc), heads(vc), heads(ks), heads(vs), heads(kw), heads(vw)
    tpos = jnp.arange(s_)
    bias_hg = bias_tab.astype(f32).reshape(N_BUCKETS, A_KV, A_GRP)

    n_cmp = (s_ - CMP_LEN) // CMP_STRIDE + 1
    cidx = np.arange(n_cmp)[:, None] * CMP_STRIDE + np.arange(CMP_LEN)[None, :]

    def compress(t, i):
        blk = (t[:, :, cidx] + cmp_pos[i]).reshape(b_, A_KV, n_cmp, CMP_LEN * A_DH)
        return jax.nn.gelu(blk @ cmp_w1[i]) @ cmp_w2[i]

    kcm = compress(kc, 0)
    vcm = compress(vc, 1)
    cend = np.arange(n_cmp) * CMP_STRIDE + CMP_LEN - 1
    dist_c = tpos[:, None] - cend[None, :]
    s_c = jnp.einsum('bkgsd,bknd->bkgsn', qh, kcm).astype(f32) + bias_hg[t5_bucket(dist_c)].transpose(2, 3, 0, 1)
    p_c = masked_softmax(s_c, dist_c >= 0)
    o_c = jnp.einsum('bkgsn,bknd->bkgsd', p_c.astype(vcm.dtype), vcm)

    n_slc = s_ // SLC_BLK
    ci = np.arange(n_cmp)[:, None] * CMP_STRIDE
    sj = np.arange(n_slc)[None, :] * SLC_BLK
    overlap = jnp.asarray(((ci < sj + SLC_BLK) & (ci + CMP_LEN > sj)).astype(np.float32))
    imp = jnp.einsum('bkgsn,nj->bksj', p_c, overlap)
    jblk = jnp.arange(n_slc)[None, :]
    cur = (tpos // SLC_BLK)[:, None]
    imp = jnp.where(jblk * SLC_BLK > tpos[:, None], -jnp.inf, imp)
    forced = (jblk == 0) | (jblk == cur) | (jblk == cur - 1)
    imp = jnp.where(forced, jnp.inf, imp)
    n_sel = min(SLC_TOPK, n_slc)
    _, sel = lax.top_k(imp, n_sel)

    ksb = ks.reshape(b_, A_KV, n_slc, SLC_BLK, A_DH)
    vsb = vs.reshape(b_, A_KV, n_slc, SLC_BLK, A_DH)
    n_qc = s_ // SLC_QBLK
    q_ch = qh.reshape(b_, A_KV, A_GRP, n_qc, SLC_QBLK, A_DH).transpose(3, 0, 1, 2, 4, 5)
    sel_ch = sel.reshape(b_, A_KV, n_qc, SLC_QBLK, n_sel).transpose(2, 0, 1, 3, 4)
    t_ch = tpos.reshape(n_qc, SLC_QBLK)
    bi = jnp.arange(b_)[:, None, None, None]
    ki = jnp.arange(A_KV)[None, :, None, None]
    ki5 = jnp.arange(A_KV)[None, :, None, None, None]
    bias_kg = bias_hg.transpose(1, 0, 2)

    def sel_block(args):
        qb, sb, tb = args
        kg = ksb[bi, ki, sb]
        vg = vsb[bi, ki, sb]
        kpos = sb[..., None] * SLC_BLK + jnp.arange(SLC_BLK)
        dist = tb[:, None, None] - kpos
        bias = jnp.moveaxis(bias_kg[ki5, t5_bucket(dist)], -1, 2)
        sc = jnp.einsum('bkgqd,bkqnld->bkgqnl', qb, kg).astype(f32) + bias
        sc = sc.reshape(b_, A_KV, A_GRP, SLC_QBLK, n_sel * SLC_BLK)
        mask = (dist >= 0).reshape(b_, A_KV, 1, SLC_QBLK, n_sel * SLC_BLK)
        pr = masked_softmax(sc, mask)
        return jnp.einsum('bkgqm,bkqmd->bkgqd', pr.astype(vg.dtype), vg.reshape(b_, A_KV, SLC_QBLK, n_sel * SLC_BLK, A_DH))

    o_s = lax.map(sel_block, (q_ch, sel_ch, t_ch))
    o_s = o_s.transpose(1, 2, 3, 0, 4, 5).reshape(b_, A_KV, A_GRP, s_, A_DH)

    n_prev = WIN_A // QBLK
    nb = s_ // QBLK
    kwb = band_blocks(kw, n_prev)
    vwb = band_blocks(vw, n_prev)
    dist_w, mask_w = band_geometry(nb, n_prev, WIN_A)
    qb = qh.reshape(b_, A_KV, A_GRP, nb, QBLK, A_DH)
    bias_w = bias_hg[t5_bucket(dist_w)].transpose(2, 3, 0, 1)[:, :, None]
    s_w = jnp.einsum('bkgnqd,bknwd->bkgnqw', qb, kwb).astype(f32) + bias_w
    p_w = masked_softmax(s_w, mask_w)
    o_w = jnp.einsum('bkgnqw,bknwd->bkgnqd', p_w.astype(vwb.dtype), vwb).reshape(b_, A_KV, A_GRP, s_, A_DH)

    gt = jax.nn.sigmoid(g.astype(f32)).reshape(b_, s_, 3, A_KV, A_GRP).transpose(2, 0, 3, 4, 1)[..., None]
    o = gt[0] * o_c.astype(f32) + gt[1] * o_s.astype(f32) + gt[2] * o_w.astype(f32)
    return o.astype(q.dtype).transpose(0, 3, 1, 2, 4).reshape(b_, s_, A_Q)


def short_conv_mixer(gate_b, gate_c, h_in, conv_w):
    u = gate_c * h_in
    y = lax.conv_general_dilated(u, conv_w[:, None, :].astype(u.dtype), window_strides=(1,),
                                 padding=[(CONV_K - 1, 0)], dimension_numbers=('NWC', 'WIO', 'NWC'),
                                 feature_group_count=CONV_DIM)
    return gate_b * y


def swa_sink_attention(q, k, v, sinks, bias_tab):
    b_, s_, _ = q.shape
    f32 = jnp.float32
    nb = s_ // QBLK
    qh = (q.reshape(b_, s_, C_KV, C_GRP, C_DH).transpose(0, 2, 3, 1, 4) * (C_DH ** -0.5)).reshape(b_, C_KV, C_GRP, nb, QBLK, C_DH)
    kb = band_blocks(k.reshape(b_, s_, C_KV, C_DH).transpose(0, 2, 1, 3), 1)
    vb = band_blocks(v.reshape(b_, s_, C_KV, C_DH).transpose(0, 2, 1, 3), 1)
    dist, mask = band_geometry(nb, 1, WIN_C)
    bias = bias_tab.astype(f32).reshape(N_BUCKETS, C_KV, C_GRP)[t5_bucket(dist)].transpose(2, 3, 0, 1)[:, :, None]
    sc = jnp.einsum('bkgnqd,bknwd->bkgnqw', qh, kb).astype(f32) + bias
    sc = jnp.where(mask, sc, -jnp.inf)
    sink = jnp.broadcast_to(sinks.astype(f32).reshape(C_KV, C_GRP, 1, 1, 1), sc.shape[:-1] + (1,))
    pr = jax.nn.softmax(jnp.concatenate([sc, sink], axis=-1), axis=-1)[..., :-1]
    o = jnp.einsum('bkgnqw,bknwd->bkgnqd', pr.astype(vb.dtype), vb).reshape(b_, C_KV, C_GRP, s_, C_DH)
    return o.transpose(0, 3, 1, 2, 4).reshape(b_, s_, C_Q)


def swiglu(x, w_gate, w_up, w_down):
    return (jax.nn.silu(x @ w_gate) * (x @ w_up)) @ w_down


def moe_swiglu(x, w_router, w_gate, w_up, w_down):
    b_, s_, d_ = x.shape
    t_ = b_ * s_
    xf = x.reshape(t_, d_)
    logits = (xf @ w_router).astype(jnp.float32)
    top_v, top_e = lax.top_k(logits, TOP_K)
    top_w = jax.nn.softmax(top_v, axis=-1)
    e_flat = top_e.reshape(-1)
    w_flat = top_w.reshape(-1)
    tok_flat = jnp.arange(t_ * TOP_K) // TOP_K
    order = jnp.argsort(e_flat)
    e_sorted = e_flat[order]
    counts = jnp.zeros((N_EXPERTS,), jnp.int32).at[e_flat].add(1)
    padded = (counts + MOE_BLK - 1) // MOE_BLK * MOE_BLK
    pad_end = jnp.cumsum(padded)
    pad_start = pad_end - padded
    start = jnp.cumsum(counts) - counts
    dest = pad_start[e_sorted] + (jnp.arange(t_ * TOP_K) - start[e_sorted])
    n_rows = -(-(t_ * TOP_K) // MOE_BLK) * MOE_BLK + N_EXPERTS * MOE_BLK
    row_tok = jnp.zeros((n_rows,), jnp.int32).at[dest].set(tok_flat[order])
    row_w = jnp.zeros((n_rows,), jnp.float32).at[dest].set(w_flat[order])
    n_blk = n_rows // MOE_BLK
    blk_expert = jnp.minimum(jnp.searchsorted(pad_end, jnp.arange(n_blk) * MOE_BLK, side='right'), N_EXPERTS - 1)
    xin = xf[row_tok].reshape(n_blk, MOE_BLK, d_)

    def expert_block(args):
        xb, e = args
        return swiglu(xb, w_gate[e], w_up[e], w_down[e])

    y_rows = lax.map(expert_block, (xin, blk_expert)).reshape(n_rows, d_)
    y = jax.ops.segment_sum(y_rows * row_w[:, None].astype(y_rows.dtype), row_tok, num_segments=t_)
    return y.reshape(b_, s_, d_)


def even_mixer(hn, w_in, cmp_pos, cmp_w1, cmp_w2, conv_w, w_out, bias_a):
    z = hn @ w_in
    cuts = np.cumsum([A_Q] + [A_KVW] * 6 + [3 * A_HEADS, CONV_DIM, CONV_DIM]).tolist()
    q, kc, vc, ks, vs, kw, vw, g, gb, gc, hc = jnp.split(z, cuts, axis=-1)
    a = nsa_attention(q, kc, vc, ks, vs, kw, vw, g, cmp_pos, cmp_w1, cmp_w2, bias_a)
    b = short_conv_mixer(gb, gc, hc, conv_w)
    return jnp.concatenate([a, b], axis=-1) @ w_out


def odd_mixer(hn, w_qkv, sinks, w_out, bias_c):
    q, k, v = jnp.split(hn @ w_qkv, [C_Q, C_Q + C_KVW], axis=-1)
    return swa_sink_attention(q, k, v, sinks, bias_c) @ w_out


def setup_inputs(seed: int = 0) -> dict:
    key = jax.random.key(seed)
    ks = jax.random.split(key, 32)
    f32 = jnp.float32

    def nrm(k, shape, scale):
        return jax.random.normal(k, shape, f32) * scale

    def gain(k, shape):
        return 1.0 + 0.05 * jax.random.normal(k, shape, f32)

    return {
        'x': nrm(ks[0], (BATCH, SEQ, D_MODEL), 1.0),
        'p': nrm(ks[1], (DEPTH, BATCH, SEQ, PLE_DIM), 1.0),
        'rel_bias': nrm(ks[2], (N_BUCKETS, N_BIAS_HEADS), 0.5),
        'norm_mix': gain(ks[3], (DEPTH, D_MODEL)),
        'norm_ffn': gain(ks[4], (DEPTH, D_MODEL)),
        'norm_ple': gain(ks[5], (DEPTH, D_MODEL)),
        'norm_final': gain(ks[6], (D_MODEL,)),
        'w_in_e': nrm(ks[7], (N_EVEN, D_MODEL, IN_E), D_MODEL ** -0.5),
        'cmp_pos': nrm(ks[8], (N_EVEN, 2, CMP_LEN, A_DH), 0.5),
        'cmp_w1': nrm(ks[9], (N_EVEN, 2, CMP_LEN * A_DH, CMP_HID), (CMP_LEN * A_DH) ** -0.5),
        'cmp_w2': nrm(ks[10], (N_EVEN, 2, CMP_HID, A_DH), CMP_HID ** -0.5),
        'conv_w': nrm(ks[11], (N_EVEN, CONV_K, CONV_DIM), CONV_K ** -0.5),
        'w_out_e': nrm(ks[12], (N_EVEN, MIX_E, D_MODEL), MIX_E ** -0.5),
        'w_gate_d': nrm(ks[13], (N_EVEN, D_MODEL, D_FF), D_MODEL ** -0.5),
        'w_up_d': nrm(ks[14], (N_EVEN, D_MODEL, D_FF), D_MODEL ** -0.5),
        'w_down_d': nrm(ks[15], (N_EVEN, D_FF, D_MODEL), D_FF ** -0.5),
        'w_qkv_o': nrm(ks[16], (N_ODD, D_MODEL, IN_O), D_MODEL ** -0.5),
        'sinks': nrm(ks[17], (N_ODD, C_HEADS), 1.0),
        'w_out_o': nrm(ks[18], (N_ODD, MIX_O, D_MODEL), MIX_O ** -0.5),
        'w_router': nrm(ks[19], (N_ODD, D_MODEL, N_EXPERTS), D_MODEL ** -0.5),
        'w_gate_m': nrm(ks[20], (N_ODD, N_EXPERTS, D_MODEL, D_FF), D_MODEL ** -0.5),
        'w_up_m': nrm(ks[21], (N_ODD, N_EXPERTS, D_MODEL, D_FF), D_MODEL ** -0.5),
        'w_down_m': nrm(ks[22], (N_ODD, N_EXPERTS, D_FF, D_MODEL), D_FF ** -0.5),
        'w_ple': nrm(ks[23], (DEPTH, PLE_DIM, D_MODEL), PLE_DIM ** -0.5),
        'w_ple_gate': nrm(ks[24], (DEPTH, D_MODEL, D_MODEL), D_MODEL ** -0.5),
    }


def reference(x, p, rel_bias, norm_mix, norm_ffn, norm_ple, norm_final, w_in_e, cmp_pos, cmp_w1, cmp_w2,
              conv_w, w_out_e, w_gate_d, w_up_d, w_down_d, w_qkv_o, sinks, w_out_o, w_router, w_gate_m,
              w_up_m, w_down_m, w_ple, w_ple_gate):
    bias_a = rel_bias[:, :A_HEADS]
    bias_c = rel_bias[:, A_HEADS:]
    h = x
    for i in range(DEPTH):
        j = i // 2
        hn = rmsnorm(h, norm_mix[i])
        if i % 2 == 0:
            h = h + even_mixer(hn, w_in_e[j], cmp_pos[j], cmp_w1[j], cmp_w2[j], conv_w[j], w_out_e[j], bias_a)
            h = h + swiglu(rmsnorm(h, norm_ffn[i]), w_gate_d[j], w_up_d[j], w_down_d[j])
        else:
            h = h + odd_mixer(hn, w_qkv_o[j], sinks[j], w_out_o[j], bias_c)
            h = h + moe_swiglu(rmsnorm(h, norm_ffn[i]), w_router[j], w_gate_m[j], w_up_m[j], w_down_m[j])
        gate = jax.nn.sigmoid(rmsnorm(h, norm_ple[i]) @ w_ple_gate[i])
        h = h + gate * (p[i].astype(h.dtype) @ w_ple[i])
    return rmsnorm(h, norm_final)
```

```python
import functools
import math

import numpy as np
import jax
import jax.numpy as jnp
from jax import lax
from jax.experimental import pallas as pl
from jax.experimental.pallas import tpu as pltpu

F32 = jnp.float32
BF16 = jnp.bfloat16

LANES = 128
VMEM_LIMIT_BYTES = 56 * 1024 * 1024

EPS = 1e-6
A_HEADS = 8
A_KV = 2
A_GRP = A_HEADS // A_KV
A_DH = 128
A_Q = A_HEADS * A_DH
A_KVW = A_KV * A_DH
CMP_LEN = 32
CMP_STRIDE = 16
SLC_BLK = 64
SLC_TOPK = 16
WIN_A = 512
CONV_DIM = 1024
CONV_K = 3
C_HEADS = 32
C_KV = 4
C_GRP = C_HEADS // C_KV
C_DH = 64
C_Q = C_HEADS * C_DH
C_KVW = C_KV * C_DH
WIN_C = 128
N_BUCKETS = 32
MAX_DIST = 1024
TOP_K = 2

QBLK = 128
FAR_REL = 8
NEG = -1e30
BIG = 1e30

ZE_Q = 0
ZE_KC, ZE_VC, ZE_KS, ZE_VS, ZE_KW, ZE_VW = 8, 10, 12, 14, 16, 18
ZE_GB, ZE_GC, ZE_HC = 20, 28, 36
ZE_GATE = 44
ZE_WIDTH = 48 * LANES


def _cparams(*sem):
    return pltpu.CompilerParams(dimension_semantics=sem, vmem_limit_bytes=VMEM_LIMIT_BYTES)


def _sigmoid(v):
    return 1.0 / (1.0 + jnp.exp(-v))


def _pick(n, pref):
    t = min(n, pref)
    while n % t:
        t -= LANES if t > LANES else 8
    return t


def _rmsnorm_kernel(x_ref, g_ref, o_ref):
    x = x_ref[...]
    y = x * lax.rsqrt(jnp.mean(x * x, axis=-1, keepdims=True) + EPS)
    o_ref[...] = (y * g_ref[...]).astype(o_ref.dtype)


def rmsnorm(x, g, out_dtype):
    t, d = x.shape
    tm = _pick(t, 512)
    return pl.pallas_call(
        _rmsnorm_kernel,
        grid=(t // tm,),
        in_specs=[pl.BlockSpec((tm, d), lambda i: (i, 0)), pl.BlockSpec((1, d), lambda i: (0, 0))],
        out_specs=pl.BlockSpec((tm, d), lambda i: (i, 0)),
        out_shape=jax.ShapeDtypeStruct((t, d), out_dtype),
        compiler_params=_cparams("parallel"),
        name="rmsnorm",
    )(x, g.reshape(1, d).astype(F32))


def _matmul_kernel(*refs, k_sizes, has_res):
    n_x = len(k_sizes)
    w_ref = refs[n_x]
    o_ref = refs[-1]
    acc = None
    off = 0
    for x_ref, k in zip(refs[:n_x], k_sizes):
        part = jnp.dot(x_ref[...], w_ref[off:off + k, :], preferred_element_type=F32)
        acc = part if acc is None else acc + part
        off += k
    if has_res:
        acc = refs[n_x + 1][...] + acc
    o_ref[...] = acc.astype(o_ref.dtype)


def matmul(xs, w, out_dtype, res=None, tm_pref=1024, tn_pref=1024):
    t = xs[0].shape[0]
    n = w.shape[1]
    k_sizes = tuple(x.shape[1] for x in xs)
    tm, tn = _pick(t, tm_pref), _pick(n, tn_pref)
    in_specs = [pl.BlockSpec((tm, k), lambda j, i: (i, 0)) for k in k_sizes]
    in_specs.append(pl.BlockSpec((w.shape[0], tn), lambda j, i: (0, j)))
    args = list(xs) + [w]
    if res is not None:
        in_specs.append(pl.BlockSpec((tm, tn), lambda j, i: (i, j)))
        args.append(res)
    return pl.pallas_call(
        functools.partial(_matmul_kernel, k_sizes=k_sizes, has_res=res is not None),
        grid=(n // tn, t // tm),
        in_specs=in_specs,
        out_specs=pl.BlockSpec((tm, tn), lambda j, i: (i, j)),
        out_shape=jax.ShapeDtypeStruct((t, n), out_dtype),
        compiler_params=_cparams("parallel", "parallel"),
        name="matmul",
    )(*args)


def _swiglu_partial(x, wg_ref, wu_ref, wd_ref):
    g = jnp.dot(x, wg_ref[...], preferred_element_type=F32)
    u = jnp.dot(x, wu_ref[...], preferred_element_type=F32)
    hid = (g * _sigmoid(g) * u).astype(BF16)
    return jnp.dot(hid, wd_ref[...], preferred_element_type=F32)


def _ffn_kernel(x_ref, wg_ref, wu_ref, wd_ref, res_ref, o_ref):
    f = pl.program_id(1)
    y = _swiglu_partial(x_ref[...], wg_ref, wu_ref, wd_ref)

    @pl.when(f == 0)
    def _():
        o_ref[...] = res_ref[...] + y

    @pl.when(f != 0)
    def _():
        o_ref[...] += y


def ffn(xn, wg, wu, wd, res, tm_pref=512, tf_pref=512):
    t, d = xn.shape
    f = wg.shape[1]
    tm, tf = _pick(t, tm_pref), _pick(f, tf_pref)
    return pl.pallas_call(
        _ffn_kernel,
        grid=(t // tm, f // tf),
        in_specs=[
            pl.BlockSpec((tm, d), lambda i, k: (i, 0)),
            pl.BlockSpec((d, tf), lambda i, k: (0, k)),
            pl.BlockSpec((d, tf), lambda i, k: (0, k)),
            pl.BlockSpec((tf, d), lambda i, k: (k, 0)),
            pl.BlockSpec((tm, d), lambda i, k: (i, 0)),
        ],
        out_specs=pl.BlockSpec((tm, d), lambda i, k: (i, 0)),
        out_shape=jax.ShapeDtypeStruct((t, d), F32),
        compiler_params=_cparams("parallel", "arbitrary"),
        name="ffn",
    )(xn, wg, wu, wd, res)


def _ple_kernel(hn_ref, p_ref, wg_ref, wp_ref, h_ref, o_ref):
    gate = _sigmoid(jnp.dot(hn_ref[...], wg_ref[...], preferred_element_type=F32))
    pe = jnp.dot(p_ref[...].astype(BF16), wp_ref[...], preferred_element_type=F32)
    o_ref[...] = h_ref[...] + gate * pe


def ple(hn, p, wg, wp, h, tm_pref=1024, tn_pref=1024):
    t, d = h.shape
    pd = p.shape[1]
    tm, tn = _pick(t, tm_pref), _pick(d, tn_pref)
    return pl.pallas_call(
        _ple_kernel,
        grid=(d // tn, t // tm),
        in_specs=[
            pl.BlockSpec((tm, d), lambda j, i: (i, 0)),
            pl.BlockSpec((tm, pd), lambda j, i: (i, 0)),
            pl.BlockSpec((d, tn), lambda j, i: (0, j)),
            pl.BlockSpec((pd, tn), lambda j, i: (0, j)),
            pl.BlockSpec((tm, tn), lambda j, i: (i, j)),
        ],
        out_specs=pl.BlockSpec((tm, tn), lambda j, i: (i, j)),
        out_shape=jax.ShapeDtypeStruct((t, d), F32),
        compiler_params=_cparams("parallel", "parallel"),
        name="ple",
    )(hn, p, wg, wp, h)


def t5_bucket(dist):
    n = jnp.maximum(jnp.asarray(dist, jnp.int32), 0)
    exact = N_BUCKETS // 2
    nf = jnp.maximum(n, 1).astype(F32)
    large = exact + (jnp.log(nf / exact) / math.log(MAX_DIST / exact) * (N_BUCKETS - exact)).astype(jnp.int32)
    return jnp.where(n < exact, n, jnp.minimum(large, N_BUCKETS - 1))


def _bias_table_kernel(idx_ref, tab_ref, o_ref, *, head0):
    h = head0 + pl.program_id(0)
    idx = idx_ref[0]
    val = jnp.full(idx.shape, tab_ref[0, h], F32)
    for b in range(1, N_BUCKETS):
        val = jnp.where(idx == b, tab_ref[b, h], val)
    o_ref[0, 0] = val


def bias_table(bucket_idx, rel_bias, head0, n_heads):
    n, r, c = bucket_idx.shape
    return pl.pallas_call(
        functools.partial(_bias_table_kernel, head0=head0),
        grid=(n_heads, n),
        in_specs=[
            pl.BlockSpec((1, r, c), lambda h, i: (i, 0, 0)),
            pl.BlockSpec(memory_space=pltpu.SMEM),
        ],
        out_specs=pl.BlockSpec((1, 1, r, c), lambda h, i: (h, i, 0, 0)),
        out_shape=jax.ShapeDtypeStruct((n_heads, n, r, c), F32),
        compiler_params=_cparams("parallel", "parallel"),
        name="bias_table",
    )(bucket_idx, rel_bias.astype(F32))


def tile_bucket_idx(n_rel):
    a = np.arange(QBLK)[None, :, None]
    c = np.arange(QBLK)[None, None, :]
    r = np.arange(n_rel)[:, None, None]
    return t5_bucket(QBLK * r + a - c)


def cmp_bucket_idx(s, n_cmp_pad):
    t = np.arange(s).reshape(s // QBLK, QBLK, 1)
    cend = (np.arange(n_cmp_pad) * CMP_STRIDE + CMP_LEN - 1)[None, None, :]
    return t5_bucket(t - cend)


def _compress_kernel(t_ref, pos_ref, w1_ref, w2_ref, o_ref, tf_ref, *, n_pad):
    s = t_ref.shape[1]
    half = CMP_LEN // 2
    tf_ref[0:s, :] = t_ref[0].astype(F32)
    tf_ref[s:s + half, :] = jnp.zeros((half, A_DH), F32)
    hid = jnp.zeros((n_pad, w1_ref.shape[2]), F32)
    for l in range(CMP_LEN):
        rows = tf_ref[pl.ds(l, n_pad, stride=CMP_STRIDE), :] + pos_ref[0, l:l + 1, :]
        hid = hid + jnp.dot(rows.astype(BF16), w1_ref[0, l * A_DH:(l + 1) * A_DH, :], preferred_element_type=F32)
    act = jax.nn.gelu(hid)
    out = jnp.dot(act.astype(BF16), w2_ref[0], preferred_element_type=F32)
    valid = lax.broadcasted_iota(jnp.int32, out.shape, 0) < n_pad - 1
    o_ref[0, 0, 0] = jnp.where(valid, out, 0.0).astype(o_ref.dtype)


def compress(ze, cmp_pos, cmp_w1, cmp_w2):
    b, s, _ = ze.shape
    n_pad = s // CMP_STRIDE
    hid = cmp_w1.shape[2]
    col0 = (ZE_KC, ZE_VC)
    return pl.pallas_call(
        functools.partial(_compress_kernel, n_pad=n_pad),
        grid=(2, b, A_KV),
        in_specs=[
            pl.BlockSpec((1, s, A_DH), lambda w, bi, k: (bi, 0, ZE_KC + 2 * w + k)),
            pl.BlockSpec((1, CMP_LEN, A_DH), lambda w, bi, k: (w, 0, 0)),
            pl.BlockSpec((1, CMP_LEN * A_DH, hid), lambda w, bi, k: (w, 0, 0)),
            pl.BlockSpec((1, hid, A_DH), lambda w, bi, k: (w, 0, 0)),
        ],
        out_specs=pl.BlockSpec((1, 1, 1, n_pad, A_DH), lambda w, bi, k: (w, bi, k, 0, 0)),
        out_shape=jax.ShapeDtypeStruct((2, b, A_KV, n_pad, A_DH), BF16),
        scratch_shapes=[pltpu.VMEM((s + CMP_LEN // 2, A_DH), F32)],
        compiler_params=_cparams("parallel", "parallel", "parallel"),
        name="nsa_compress",
    )(ze, cmp_pos.astype(F32), cmp_w1.astype(BF16), cmp_w2.astype(BF16))


def _masked_softmax(s, mask):
    sm = jnp.where(mask, s, NEG)
    m = jnp.max(sm, axis=-1, keepdims=True)
    e = jnp.where(mask, jnp.exp(sm - m), 0.0)
    return e / jnp.maximum(jnp.sum(e, axis=-1, keepdims=True), 1e-30)


def _nsa_kernel(q_ref, kcm_ref, vcm_ref, ks_ref, vs_ref, kw_ref, vw_ref, g_ref, tw_ref, tc_ref, o_ref,
                m_ref, l_ref, acc_ref):
    i = pl.program_id(2)
    tq = QBLK
    n_cmp = kcm_ref.shape[3]
    s_len = ks_ref.shape[1]
    nt = (((1,), (1,)), ((), ()))

    q = q_ref[0]
    qs = jnp.concatenate([q[:, g * A_DH:(g + 1) * A_DH] for g in range(A_GRP)], axis=0)
    qs = (qs.astype(F32) * (A_DH ** -0.5)).astype(BF16)
    t_col = i * tq + lax.broadcasted_iota(jnp.int32, (tq, 1), 0)

    s_c = lax.dot_general(qs, kcm_ref[0, 0, 0], nt, preferred_element_type=F32)
    s_c = s_c.reshape(A_GRP, tq, n_cmp) + tc_ref[0, :, 0]
    n_row = lax.broadcasted_iota(jnp.int32, (tq, n_cmp), 1)
    mask_c = (t_col >= n_row * CMP_STRIDE + (CMP_LEN - 1)) & (n_row < n_cmp - 1)
    p_c = _masked_softmax(s_c, mask_c[None])
    o_c = jnp.dot(p_c.reshape(A_GRP * tq, n_cmp).astype(BF16), vcm_ref[0, 0, 0], preferred_element_type=F32)

    p_sum = p_c[0]
    for g in range(1, A_GRP):
        p_sum = p_sum + p_c[g]
    ov_n = lax.broadcasted_iota(jnp.int32, (n_cmp, LANES), 0) * CMP_STRIDE
    ov_j = lax.broadcasted_iota(jnp.int32, (n_cmp, LANES), 1) * SLC_BLK
    overlap = ((ov_n < ov_j + SLC_BLK) & (ov_n + CMP_LEN > ov_j)).astype(F32)
    imp = jnp.dot(p_sum, overlap, preferred_element_type=F32, precision=lax.Precision.HIGHEST)
    imp_t = imp.T
    n_slc = s_len // SLC_BLK
    jb = lax.broadcasted_iota(jnp.int32, (LANES, tq), 0)
    t_row = i * tq + lax.broadcasted_iota(jnp.int32, (LANES, tq), 1)
    cur = t_row // SLC_BLK
    imp_t = jnp.where((jb * SLC_BLK > t_row) | (jb >= n_slc), -BIG, imp_t)
    imp_t = jnp.where((jb == 0) | (jb == cur) | (jb == cur - 1), BIG, imp_t)
    rank = jnp.zeros((LANES, tq), F32)
    for j2 in range(min(n_slc, LANES)):
        other = imp_t[j2:j2 + 1, :]
        ahead = (other > imp_t) | ((other == imp_t) & (jb > j2))
        rank = rank + jnp.where(ahead, 1.0, 0.0)
    sel_t = jnp.where((rank < float(min(SLC_TOPK, n_slc))) & (jb < n_slc), 1.0, 0.0)
    sel = sel_t.T.astype(BF16)

    n_w = WIN_A // tq + 1
    jw0 = jnp.maximum(i - WIN_A // tq, 0)
    w_start = pl.multiple_of(jw0 * tq, tq)
    k_w = kw_ref[0, pl.ds(w_start, n_w * tq), :]
    v_w = vw_ref[0, pl.ds(w_start, n_w * tq), :]
    s_w = lax.dot_general(qs, k_w, nt, preferred_element_type=F32).reshape(A_GRP, tq, n_w * tq)
    bias_w = jnp.concatenate([tw_ref[0, jnp.clip(i - jw0 - cb, 0, FAR_REL)] for cb in range(n_w)], axis=-1)
    dist_w = t_col - (w_start + lax.broadcasted_iota(jnp.int32, (tq, n_w * tq), 1))
    mask_w = (dist_w >= 0) & (dist_w < WIN_A)
    p_w = _masked_softmax(s_w + bias_w, mask_w[None])
    o_w = jnp.dot(p_w.reshape(A_GRP * tq, n_w * tq).astype(BF16), v_w, preferred_element_type=F32)

    ck = 2 * tq
    m_ref[...] = jnp.full(m_ref.shape, NEG, F32)
    l_ref[...] = jnp.zeros(l_ref.shape, F32)
    acc_ref[...] = jnp.zeros(acc_ref.shape, F32)
    blk_of_key = (lax.broadcasted_iota(jnp.int32, (LANES, ck), 0)
                  - lax.broadcasted_iota(jnp.int32, (LANES, ck), 1) // SLC_BLK)
    key_iota = lax.broadcasted_iota(jnp.int32, (tq, ck), 1)

    def chunk(c, carry):
        start = pl.multiple_of(c * ck, ck)
        k_s = ks_ref[0, pl.ds(start, ck), :]
        v_s = vs_ref[0, pl.ds(start, ck), :]
        s_s = lax.dot_general(qs, k_s, nt, preferred_element_type=F32).reshape(A_GRP, tq, ck)
        bias = jnp.concatenate([tw_ref[0, jnp.clip(i - 2 * c - cb, 0, FAR_REL)] for cb in range(2)], axis=-1)
        expand = jnp.where(blk_of_key == (ck // SLC_BLK) * c, 1.0, 0.0).astype(BF16)
        chosen = jnp.dot(sel, expand, preferred_element_type=F32) > 0.5
        mask = (chosen & (start + key_iota <= t_col))[None]
        sm = jnp.where(mask, s_s + bias, NEG)
        m_old = m_ref[...]
        m_new = jnp.maximum(m_old, jnp.max(sm, axis=-1, keepdims=True).reshape(A_GRP * tq, 1))
        p = jnp.where(mask, jnp.exp(sm - m_new.reshape(A_GRP, tq, 1)), 0.0).reshape(A_GRP * tq, ck)
        alpha = jnp.exp(m_old - m_new)
        l_ref[...] = alpha * l_ref[...] + jnp.sum(p, axis=-1, keepdims=True)
        acc_ref[...] = alpha * acc_ref[...] + jnp.dot(p.astype(BF16), v_s, preferred_element_type=F32)
        m_ref[...] = m_new
        return carry

    lax.fori_loop(0, i // 2 + 1, chunk, 0)
    o_s = acc_ref[...] / jnp.maximum(l_ref[...], 1e-30)

    gate = _sigmoid(g_ref[0].astype(F32))
    outs = []
    for g in range(A_GRP):
        rows = slice(g * tq, (g + 1) * tq)
        outs.append(gate[:, g:g + 1] * o_c[rows]
                    + gate[:, A_GRP + g:A_GRP + g + 1] * o_s[rows]
                    + gate[:, 2 * A_GRP + g:2 * A_GRP + g + 1] * o_w[rows])
    o_ref[0] = jnp.concatenate(outs, axis=-1).astype(o_ref.dtype)


def nsa_attention(ze, kvcm, tw, tc):
    b, s, _ = ze.shape
    n_cmp = kvcm.shape[3]
    gq = A_GRP * QBLK

    def kv_spec(col):
        return pl.BlockSpec((1, s, A_DH), lambda bi, k, i: (bi, 0, col + k))

    return pl.pallas_call(
        _nsa_kernel,
        grid=(b, A_KV, s // QBLK),
        in_specs=[
            pl.BlockSpec((1, QBLK, A_GRP * A_DH), lambda bi, k, i: (bi, i, k)),
            pl.BlockSpec((1, 1, 1, n_cmp, A_DH), lambda bi, k, i: (0, bi, k, 0, 0)),
            pl.BlockSpec((1, 1, 1, n_cmp, A_DH), lambda bi, k, i: (1, bi, k, 0, 0)),
            kv_spec(ZE_KS), kv_spec(ZE_VS), kv_spec(ZE_KW), kv_spec(ZE_VW),
            pl.BlockSpec((1, QBLK, LANES), lambda bi, k, i: (bi, i, ZE_GATE + k)),
            pl.BlockSpec((1, FAR_REL + 1, A_GRP, QBLK, QBLK), lambda bi, k, i: (k, 0, 0, 0, 0)),
            pl.BlockSpec((1, A_GRP, 1, QBLK, n_cmp), lambda bi, k, i: (k, 0, i, 0, 0)),
        ],
        out_specs=pl.BlockSpec((1, QBLK, A_GRP * A_DH), lambda bi, k, i: (bi, i, k)),
        out_shape=jax.ShapeDtypeStruct((b, s, A_Q), BF16),
        scratch_shapes=[pltpu.VMEM((gq, 1), F32), pltpu.VMEM((gq, 1), F32), pltpu.VMEM((gq, A_DH), F32)],
        compiler_params=_cparams("parallel", "parallel", "arbitrary"),
        name="nsa_attention",
    )(ze, kvcm, kvcm, ze, ze, ze, ze, ze, tw, tc)


def _conv_kernel(gb_ref, gc_ref, hc_ref, gcp_ref, hcp_ref, w_ref, o_ref):
    i = pl.program_id(1)
    u = gc_ref[0].astype(F32) * hc_ref[0].astype(F32)
    prev = gcp_ref[0].astype(F32) * hcp_ref[0].astype(F32)
    prev = jnp.where(i > 0, prev, 0.0)
    n_prev = prev.shape[0]
    p1 = prev[n_prev - 1:n_prev, :]
    p2 = prev[n_prev - 2:n_prev - 1, :]
    row = lax.broadcasted_iota(jnp.int32, u.shape, 0)
    u1 = jnp.where(row == 0, p1, pltpu.roll(u, 1, 0))
    u2 = jnp.where(row == 0, p2, jnp.where(row == 1, p1, pltpu.roll(u, 2, 0)))
    w = w_ref[...]
    y = w[0:1, :] * u2 + w[1:2, :] * u1 + w[2:3, :] * u
    o_ref[0] = (gb_ref[0].astype(F32) * y).astype(o_ref.dtype)


def short_conv(ze, conv_w, ts_pref=512, halo=16):
    b, s, _ = ze.shape
    ts = _pick(s, ts_pref)
    cw = 4 * LANES
    nc = CONV_DIM // cw

    def cur(col):
        return pl.BlockSpec((1, ts, cw), lambda bi, i, c: (bi, i, col * LANES // cw + c))

    def prev(col):
        return pl.BlockSpec((1, halo, cw), lambda bi, i, c: (bi, jnp.maximum(i * (ts // halo) - 1, 0), col * LANES // cw + c))

    return pl.pallas_call(
        _conv_kernel,
        grid=(b, s // ts, nc),
        in_specs=[cur(ZE_GB), cur(ZE_GC), cur(ZE_HC), prev(ZE_GC), prev(ZE_HC),
                  pl.BlockSpec((CONV_K, cw), lambda bi, i, c: (0, c))],
        out_specs=pl.BlockSpec((1, ts, cw), lambda bi, i, c: (bi, i, c)),
        out_shape=jax.ShapeDtypeStruct((b, s, CONV_DIM), BF16),
        compiler_params=_cparams("parallel", "parallel", "parallel"),
        name="short_conv",
    )(ze, ze, ze, ze, ze, conv_w.astype(F32))


def _swa_kernel(q_ref, kp_ref, kc_ref, vp_ref, vc_ref, sink_ref, tb_ref, o_ref):
    i = pl.program_id(1)
    tq = QBLK
    nt = (((1,), (1,)), ((), ()))
    q = q_ref[0]
    a_row = lax.broadcasted_iota(jnp.int32, (tq, 2 * tq), 0)
    c_col = lax.broadcasted_iota(jnp.int32, (tq, 2 * tq), 1)
    dist = tq + a_row - c_col
    mask = ((dist >= 0) & (dist < WIN_C) & ((c_col >= tq) | (i > 0)))[None]
    outs = []
    for kv in range(C_KV):
        cols = slice(kv * C_DH, (kv + 1) * C_DH)
        k = jnp.concatenate([kp_ref[0][:, cols], kc_ref[0][:, cols]], axis=0)
        v = jnp.concatenate([vp_ref[0][:, cols], vc_ref[0][:, cols]], axis=0)
        qs = jnp.concatenate([q[:, (kv * C_GRP + g) * C_DH:(kv * C_GRP + g + 1) * C_DH] for g in range(C_GRP)], axis=0)
        qs = (qs.astype(F32) * (C_DH ** -0.5)).astype(BF16)
        s = lax.dot_general(qs, k, nt, preferred_element_type=F32).reshape(C_GRP, tq, 2 * tq)
        bias = jnp.concatenate([tb_ref[kv * C_GRP:(kv + 1) * C_GRP, 1], tb_ref[kv * C_GRP:(kv + 1) * C_GRP, 0]], axis=-1)
        sm = jnp.where(mask, s + bias, NEG)
        sink = jnp.concatenate([jnp.full((1, tq, 1), sink_ref[kv * C_GRP + g], F32) for g in range(C_GRP)], axis=0)
        m = jnp.maximum(jnp.max(sm, axis=-1, keepdims=True), sink)
        e = jnp.where(mask, jnp.exp(sm - m), 0.0)
        p = e / (jnp.sum(e, axis=-1, keepdims=True) + jnp.exp(sink - m))
        o = jnp.dot(p.reshape(C_GRP * tq, 2 * tq).astype(BF16), v, preferred_element_type=F32)
        outs.extend(o[g * tq:(g + 1) * tq] for g in range(C_GRP))
    o_ref[0] = jnp.concatenate(outs, axis=-1).astype(o_ref.dtype)


def swa_attention(zo, sinks, tb):
    b, s, _ = zo.shape
    kcol = C_Q // C_KVW
    prev = lambda bi, i: jnp.maximum(i - 1, 0)
    return pl.pallas_call(
        _swa_kernel,
        grid=(b, s // QBLK),
        in_specs=[
            pl.BlockSpec((1, QBLK, C_Q), lambda bi, i: (bi, i, 0)),
            pl.BlockSpec((1, QBLK, C_KVW), lambda bi, i: (bi, prev(bi, i), kcol)),
            pl.BlockSpec((1, QBLK, C_KVW), lambda bi, i: (bi, i, kcol)),
            pl.BlockSpec((1, QBLK, C_KVW), lambda bi, i: (bi, prev(bi, i), kcol + 1)),
            pl.BlockSpec((1, QBLK, C_KVW), lambda bi, i: (bi, i, kcol + 1)),
            pl.BlockSpec(memory_space=pltpu.SMEM),
            pl.BlockSpec((C_HEADS, 2, QBLK, QBLK), lambda bi, i: (0, 0, 0, 0)),
        ],
        out_specs=pl.BlockSpec((1, QBLK, C_Q), lambda bi, i: (bi, i, 0)),
        out_shape=jax.ShapeDtypeStruct((b, s, C_Q), BF16),
        compiler_params=_cparams("parallel", "parallel"),
        name="swa_attention",
    )(zo, zo, zo, zo, zo, sinks.astype(F32), tb)


def _router_kernel(h_ref, g_ref, wr_ref, xn_ref, idx_ref, wt_ref, *, n_experts):
    x = h_ref[...]
    xn = x * lax.rsqrt(jnp.mean(x * x, axis=-1, keepdims=True) + EPS) * g_ref[...]
    xn_ref[...] = xn
    logits = jnp.dot(xn, wr_ref[...], preferred_element_type=F32, precision=lax.Precision.HIGHEST)
    lane = lax.broadcasted_iota(jnp.int32, logits.shape, 1)
    lg = jnp.where(lane < n_experts, logits, NEG)
    m1 = jnp.max(lg, axis=-1, keepdims=True)
    i1 = jnp.min(jnp.where(lg == m1, lane, LANES), axis=-1, keepdims=True)
    lg2 = jnp.where(lane == i1, NEG, lg)
    m2 = jnp.max(lg2, axis=-1, keepdims=True)
    i2 = jnp.min(jnp.where(lg2 == m2, lane, LANES), axis=-1, keepdims=True)
    e2 = jnp.exp(m2 - m1)
    idx_ref[...] = jnp.where(lane == 0, i1, jnp.where(lane == 1, i2, 0))
    wt_ref[...] = jnp.where(lane == 0, 1.0 / (1.0 + e2), jnp.where(lane == 1, e2 / (1.0 + e2), 0.0))


def router(h, g, w_router):
    t, d = h.shape
    e = w_router.shape[1]
    tm = _pick(t, 512)
    wr = jnp.zeros((d, LANES), F32).at[:, :e].set(w_router.astype(F32))
    row = lambda i: (i, 0)
    return pl.pallas_call(
        functools.partial(_router_kernel, n_experts=e),
        grid=(t // tm,),
        in_specs=[pl.BlockSpec((tm, d), row), pl.BlockSpec((1, d), lambda i: (0, 0)),
                  pl.BlockSpec((d, LANES), lambda i: (0, 0))],
        out_specs=[pl.BlockSpec((tm, d), row), pl.BlockSpec((tm, LANES), row), pl.BlockSpec((tm, LANES), row)],
        out_shape=[jax.ShapeDtypeStruct((t, d), F32), jax.ShapeDtypeStruct((t, LANES), jnp.int32),
                   jax.ShapeDtypeStruct((t, LANES), F32)],
        compiler_params=_cparams("parallel"),
        name="moe_router",
    )(h, g.reshape(1, d).astype(F32), wr)


def _row_copy(src_hbm, dst_ref, sem, src_row, dst_row):
    return pltpu.make_async_copy(src_hbm.at[pl.ds(src_row, 1), :], dst_ref.at[pl.ds(dst_row, 1), :], sem)


def _gather_kernel(tok_ref, x_hbm, o_ref, sem):
    rows = o_ref.shape[0]

    def start(r, c):
        _row_copy(x_hbm, o_ref, sem, tok_ref[0, 0, r], r).start()
        return c

    def wait(r, c):
        _row_copy(x_hbm, o_ref, sem, 0, r).wait()
        return c

    lax.fori_loop(0, rows, start, 0)
    lax.fori_loop(0, rows, wait, 0)


def gather_rows(x, row_tok, tm):
    n_rows = row_tok.shape[0]
    d = x.shape[1]
    return pl.pallas_call(
        _gather_kernel,
        grid=(n_rows // tm,),
        in_specs=[pl.BlockSpec((1, 1, tm), lambda i: (i, 0, 0), memory_space=pltpu.SMEM),
                  pl.BlockSpec(memory_space=pl.ANY)],
        out_specs=pl.BlockSpec((tm, d), lambda i: (i, 0)),
        out_shape=jax.ShapeDtypeStruct((n_rows, d), x.dtype),
        scratch_shapes=[pltpu.SemaphoreType.DMA(())],
        compiler_params=_cparams("arbitrary"),
        name="moe_gather",
    )(row_tok.reshape(n_rows // tm, 1, tm), x)


def _combine_kernel(p0_ref, p1_ref, y_hbm, h_ref, o_ref, a_ref, b_ref, sem):
    rows = o_ref.shape[0]

    def start(r, c):
        _row_copy(y_hbm, a_ref, sem.at[0], p0_ref[0, 0, r], r).start()
        _row_copy(y_hbm, b_ref, sem.at[1], p1_ref[0, 0, r], r).start()
        return c

    def wait(r, c):
        _row_copy(y_hbm, a_ref, sem.at[0], 0, r).wait()
        _row_copy(y_hbm, b_ref, sem.at[1], 0, r).wait()
        return c

    lax.fori_loop(0, rows, start, 0)
    lax.fori_loop(0, rows, wait, 0)
    o_ref[...] = h_ref[...] + a_ref[...] + b_ref[...]


def combine_rows(y_rows, pos0, pos1, h, tm):
    t, d = h.shape
    idx_spec = pl.BlockSpec((1, 1, tm), lambda i: (i, 0, 0), memory_space=pltpu.SMEM)
    return pl.pallas_call(
        _combine_kernel,
        grid=(t // tm,),
        in_specs=[idx_spec, idx_spec, pl.BlockSpec(memory_space=pl.ANY), pl.BlockSpec((tm, d), lambda i: (i, 0))],
        out_specs=pl.BlockSpec((tm, d), lambda i: (i, 0)),
        out_shape=jax.ShapeDtypeStruct((t, d), F32),
        scratch_shapes=[pltpu.VMEM((tm, d), F32), pltpu.VMEM((tm, d), F32), pltpu.SemaphoreType.DMA((2,))],
        compiler_params=_cparams("arbitrary"),
        name="moe_combine",
    )(pos0.reshape(t // tm, 1, tm), pos1.reshape(t // tm, 1, tm), y_rows, h)


def _moe_ffn_kernel(be_ref, nu_ref, x_ref, wg_ref, wu_ref, wd_ref, rw_ref, o_ref, xb_ref, acc_ref):
    i = pl.program_id(0)
    f = pl.program_id(1)
    nf = pl.num_programs(1)

    @pl.when(i < nu_ref[0])
    def _():
        @pl.when(f == 0)
        def _():
            xb_ref[...] = x_ref[...].astype(BF16)

        y = _swiglu_partial(xb_ref[...], wg_ref, wu_ref, wd_ref)

        @pl.when(f == 0)
        def _():
            acc_ref[...] = y

        @pl.when(f != 0)
        def _():
            acc_ref[...] += y

        @pl.when(f == nf - 1)
        def _():
            o_ref[...] = acc_ref[...] * rw_ref[...]

    @pl.when((i >= nu_ref[0]) & (f == nf - 1))
    def _():
        o_ref[...] = jnp.zeros(o_ref.shape, o_ref.dtype)


def moe_ffn(xs, blk_expert, n_used, row_w, wg, wu, wd, tm, tf_pref=512):
    n_rows, d = xs.shape
    f = wg.shape[2]
    tf = _pick(f, tf_pref)
    nf = f // tf

    def used(i, nu):
        return jnp.minimum(i, nu[0] - 1)

    def fcol(i, k, nu):
        return jnp.where(i < nu[0], k, nf - 1)

    grid_spec = pltpu.PrefetchScalarGridSpec(
        num_scalar_prefetch=2,
        grid=(n_rows // tm, nf),
        in_specs=[
            pl.BlockSpec((tm, d), lambda i, k, be, nu: (used(i, nu), 0)),
            pl.BlockSpec((None, d, tf), lambda i, k, be, nu: (be[used(i, nu)], 0, fcol(i, k, nu))),
            pl.BlockSpec((None, d, tf), lambda i, k, be, nu: (be[used(i, nu)], 0, fcol(i, k, nu))),
            pl.BlockSpec((None, tf, d), lambda i, k, be, nu: (be[used(i, nu)], fcol(i, k, nu), 0)),
            pl.BlockSpec((tm, 1), lambda i, k, be, nu: (used(i, nu), 0)),
        ],
        out_specs=pl.BlockSpec((tm, d), lambda i, k, be, nu: (i, 0)),
        scratch_shapes=[pltpu.VMEM((tm, d), BF16), pltpu.VMEM((tm, d), F32)],
    )
    return pl.pallas_call(
        _moe_ffn_kernel,
        grid_spec=grid_spec,
        out_shape=jax.ShapeDtypeStruct((n_rows, d), F32),
        compiler_params=_cparams("arbitrary", "arbitrary"),
        name="moe_ffn",
    )(blk_expert, n_used, xs, wg, wu, wd, row_w.reshape(n_rows, 1))


def moe_layer(h, g, w_router, wg, wu, wd, tm_pref=512):
    t, d = h.shape
    n_exp = w_router.shape[1]
    tm = _pick(t, tm_pref)
    xn, idx, wt = router(h, g, w_router)
    e_flat = idx[:, :TOP_K].reshape(-1)
    w_flat = wt[:, :TOP_K].reshape(-1)
    onehot = (e_flat[:, None] == jnp.arange(n_exp)[None, :]).astype(jnp.int32)
    csum = jnp.cumsum(onehot, axis=0)
    rank = jnp.sum((csum - onehot) * onehot, axis=1)
    counts = csum[-1]
    padded = (counts + tm - 1) // tm * tm
    pad_end = jnp.cumsum(padded)
    dest = (pad_end - padded)[e_flat] + rank
    n_rows = t * TOP_K + n_exp * tm
    row_tok = jnp.zeros((n_rows,), jnp.int32).at[dest].set(jnp.arange(t * TOP_K, dtype=jnp.int32) // TOP_K)
    row_w = jnp.zeros((n_rows,), F32).at[dest].set(w_flat)
    n_blk = n_rows // tm
    blk_expert = jnp.minimum(jnp.searchsorted(pad_end, jnp.arange(n_blk) * tm, side='right'), n_exp - 1).astype(jnp.int32)
    n_used = (pad_end[-1:] // tm).astype(jnp.int32)

    xs = gather_rows(xn, row_tok, tm)
    y_rows = moe_ffn(xs, blk_expert, n_used, row_w, wg.astype(BF16), wu.astype(BF16), wd.astype(BF16), tm)
    pos = dest.reshape(t, TOP_K).astype(jnp.int32)
    return combine_rows(y_rows, pos[:, 0], pos[:, 1], h, tm)


def _even_in_proj_weight(w_in):
    d = w_in.shape[0]
    g0 = A_Q + 6 * A_KVW
    gates = w_in[:, g0:g0 + 3 * A_HEADS].reshape(d, 3, A_KV, A_GRP)
    blocks = [w_in[:, :g0], w_in[:, g0 + 3 * A_HEADS:]]
    for k in range(A_KV):
        gk = gates[:, :, k, :].reshape(d, 3 * A_GRP)
        blocks.append(jnp.pad(gk, ((0, 0), (0, LANES - 3 * A_GRP))))
    w = jnp.concatenate(blocks, axis=1)
    return jnp.pad(w, ((0, 0), (0, ZE_WIDTH - w.shape[1]))).astype(BF16)


def kernel(x, p, rel_bias, norm_mix, norm_ffn, norm_ple, norm_final, w_in_e, cmp_pos, cmp_w1, cmp_w2, conv_w,
           w_out_e, w_gate_d, w_up_d, w_down_d, w_qkv_o, sinks, w_out_o, w_router, w_gate_m, w_up_m, w_down_m,
           w_ple, w_ple_gate):
    b, s, d = x.shape
    t = b * s
    depth = norm_mix.shape[0]
    n_cmp_pad = s // CMP_STRIDE

    tw = bias_table(tile_bucket_idx(FAR_REL + 1), rel_bias, 0, A_HEADS)
    tw = tw.reshape(A_KV, A_GRP, FAR_REL + 1, QBLK, QBLK).transpose(0, 2, 1, 3, 4)
    tc = bias_table(cmp_bucket_idx(s, n_cmp_pad), rel_bias, 0, A_HEADS)
    tc = tc.reshape(A_KV, A_GRP, s // QBLK, QBLK, n_cmp_pad)
    tb = bias_table(tile_bucket_idx(2), rel_bias, A_HEADS, C_HEADS)

    h = x.reshape(t, d).astype(F32)
    for i in range(depth):
        j = i // 2
        hn = rmsnorm(h, norm_mix[i], BF16)
        if i % 2 == 0:
            ze = matmul([hn], _even_in_proj_weight(w_in_e[j]), BF16).reshape(b, s, ZE_WIDTH)
            kvcm = compress(ze, cmp_pos[j], cmp_w1[j], cmp_w2[j])
            att = nsa_attention(ze, kvcm, tw, tc).reshape(t, A_Q)
            cnv = short_conv(ze, conv_w[j]).reshape(t, CONV_DIM)
            h = matmul([att, cnv], w_out_e[j].astype(BF16), F32, res=h)
            hn = rmsnorm(h, norm_ffn[i], BF16)
            h = ffn(hn, w_gate_d[j].astype(BF16), w_up_d[j].astype(BF16), w_down_d[j].astype(BF16), h)
        else:
            zo = matmul([hn], w_qkv_o[j].astype(BF16), BF16, tn_pref=1280).reshape(b, s, -1)
            att = swa_attention(zo, sinks[j], tb).reshape(t, C_Q)
            h = matmul([att], w_out_o[j].astype(BF16), F32, res=h)
            h = moe_layer(h, norm_ffn[i], w_router[j], w_gate_m[j], w_up_m[j], w_down_m[j])
        hn = rmsnorm(h, norm_ple[i], BF16)
        h = ple(hn, p[i].reshape(t, -1), w_ple_gate[i].astype(BF16), w_ple[i].astype(BF16), h)
    return rmsnorm(h, norm_final, x.dtype).reshape(b, s, d)
```

```python
import functools
import math

import numpy as np
import jax
import jax.numpy as jnp
from jax import lax
from jax.experimental import pallas as pl
from jax.experimental.pallas import tpu as pltpu

F32 = jnp.float32
BF16 = jnp.bfloat16

LANES = 128
VMEM_LIMIT_BYTES = 56 * 1024 * 1024

EPS = 1e-6
A_HEADS = 8
A_KV = 2
A_GRP = A_HEADS // A_KV
A_DH = 128
A_Q = A_HEADS * A_DH
A_KVW = A_KV * A_DH
CMP_LEN = 32
CMP_STRIDE = 16
SLC_BLK = 64
SLC_TOPK = 16
WIN_A = 512
CONV_DIM = 1024
CONV_K = 3
C_HEADS = 32
C_KV = 4
C_GRP = C_HEADS // C_KV
C_DH = 64
C_Q = C_HEADS * C_DH
C_KVW = C_KV * C_DH
WIN_C = 128
N_BUCKETS = 32
MAX_DIST = 1024
TOP_K = 2

QBLK = 128
FAR_REL = 8
NEG = -1e30
BIG = 1e30

ZE_Q = 0
ZE_KC, ZE_VC, ZE_KS, ZE_VS, ZE_KW, ZE_VW = 8, 10, 12, 14, 16, 18
ZE_GB, ZE_GC, ZE_HC = 20, 28, 36
ZE_GATE = 44
ZE_WIDTH = 48 * LANES


def _cparams(*sem):
    return pltpu.CompilerParams(dimension_semantics=sem, vmem_limit_bytes=VMEM_LIMIT_BYTES)


def _sigmoid(v):
    return 1.0 / (1.0 + jnp.exp(-v))


def _pick(n, pref):
    t = min(n, pref)
    while n % t:
        t -= LANES if t > LANES else 8
    return t


def _rmsnorm_kernel(x_ref, g_ref, o_ref):
    x = x_ref[...]
    y = x * lax.rsqrt(jnp.mean(x * x, axis=-1, keepdims=True) + EPS)
    o_ref[...] = (y * g_ref[...]).astype(o_ref.dtype)


def rmsnorm(x, g, out_dtype):
    t, d = x.shape
    tm = _pick(t, 512)
    return pl.pallas_call(
        _rmsnorm_kernel,
        grid=(t // tm,),
        in_specs=[pl.BlockSpec((tm, d), lambda i: (i, 0)), pl.BlockSpec((1, d), lambda i: (0, 0))],
        out_specs=pl.BlockSpec((tm, d), lambda i: (i, 0)),
        out_shape=jax.ShapeDtypeStruct((t, d), out_dtype),
        compiler_params=_cparams("parallel"),
        name="rmsnorm",
    )(x, g.reshape(1, d).astype(F32))


def _dense_groups(n_blk, layer):
    grp = jnp.full((n_blk,), layer, jnp.int32)
    first = (jnp.arange(n_blk) == 0).astype(jnp.int32)
    return grp, first, jnp.full((1,), n_blk, jnp.int32)


def _gmm_kernel(grp_ref, first_ref, nu_ref, *refs, k_sizes, has_res, has_scale):
    n_x = len(k_sizes)
    w_ref = refs[n_x]
    o_ref, wb_ref = refs[-2], refs[-1]
    i = pl.program_id(1)

    @pl.when(first_ref[i] == 1)
    def _():
        wb_ref[...] = w_ref[...].astype(BF16)

    @pl.when(i < nu_ref[0])
    def _():
        acc = None
        off = 0
        for x_ref, k in zip(refs[:n_x], k_sizes):
            part = jnp.dot(x_ref[...], wb_ref[off:off + k, :], preferred_element_type=F32)
            acc = part if acc is None else acc + part
            off += k
        if has_res:
            acc = refs[n_x + 1][...] + acc
        if has_scale:
            acc = acc * refs[n_x + 1 + has_res][...]
        o_ref[...] = acc.astype(o_ref.dtype)

    @pl.when(i >= nu_ref[0])
    def _():
        o_ref[...] = jnp.zeros(o_ref.shape, o_ref.dtype)


def grouped_matmul(xs, w, groups, out_dtype, tm, tn_pref, res=None, scale=None):
    t = xs[0].shape[0]
    kw, n = w.shape[1], w.shape[2]
    k_sizes = tuple(x.shape[1] for x in xs)
    tn = _pick(n, tn_pref)
    in_specs = [pl.BlockSpec((tm, k), lambda j, i, g, f, nu: (i, 0)) for k in k_sizes]
    in_specs.append(pl.BlockSpec((None, kw, tn), lambda j, i, g, f, nu: (g[i], 0, j)))
    args = list(xs) + [w]
    if res is not None:
        in_specs.append(pl.BlockSpec((tm, tn), lambda j, i, g, f, nu: (i, j)))
        args.append(res)
    if scale is not None:
        in_specs.append(pl.BlockSpec((tm, 1), lambda j, i, g, f, nu: (i, 0)))
        args.append(scale)
    grid_spec = pltpu.PrefetchScalarGridSpec(
        num_scalar_prefetch=3,
        grid=(n // tn, t // tm),
        in_specs=in_specs,
        out_specs=pl.BlockSpec((tm, tn), lambda j, i, g, f, nu: (i, j)),
        scratch_shapes=[pltpu.VMEM((kw, tn), BF16)],
    )
    return pl.pallas_call(
        functools.partial(_gmm_kernel, k_sizes=k_sizes, has_res=res is not None, has_scale=scale is not None),
        grid_spec=grid_spec,
        out_shape=jax.ShapeDtypeStruct((t, n), out_dtype),
        compiler_params=_cparams("arbitrary", "arbitrary"),
        name="matmul",
    )(*groups, *args)


def _glu_kernel(grp_ref, first_ref, nu_ref, x_ref, wg_ref, wu_ref, o_ref, wgb_ref, wub_ref):
    i = pl.program_id(1)

    @pl.when(first_ref[i] == 1)
    def _():
        wgb_ref[...] = wg_ref[...].astype(BF16)
        wub_ref[...] = wu_ref[...].astype(BF16)

    @pl.when(i < nu_ref[0])
    def _():
        x = x_ref[...]
        g = jnp.dot(x, wgb_ref[...], preferred_element_type=F32)
        u = jnp.dot(x, wub_ref[...], preferred_element_type=F32)
        o_ref[...] = (g * _sigmoid(g) * u).astype(o_ref.dtype)

    @pl.when(i >= nu_ref[0])
    def _():
        o_ref[...] = jnp.zeros(o_ref.shape, o_ref.dtype)


def grouped_glu(x, wg, wu, groups, tm, tf_pref=512):
    t, d = x.shape
    f = wg.shape[2]
    tf = _pick(f, tf_pref)
    w_spec = pl.BlockSpec((None, d, tf), lambda j, i, g, fl, nu: (g[i], 0, j))
    grid_spec = pltpu.PrefetchScalarGridSpec(
        num_scalar_prefetch=3,
        grid=(f // tf, t // tm),
        in_specs=[pl.BlockSpec((tm, d), lambda j, i, g, fl, nu: (i, 0)), w_spec, w_spec],
        out_specs=pl.BlockSpec((tm, tf), lambda j, i, g, fl, nu: (i, j)),
        scratch_shapes=[pltpu.VMEM((d, tf), BF16), pltpu.VMEM((d, tf), BF16)],
    )
    return pl.pallas_call(
        _glu_kernel,
        grid_spec=grid_spec,
        out_shape=jax.ShapeDtypeStruct((t, f), BF16),
        compiler_params=_cparams("arbitrary", "arbitrary"),
        name="glu",
    )(*groups, x, wg, wu)


def _ple_kernel(hn_ref, p_ref, wg_ref, wp_ref, h_ref, o_ref, wgb_ref, wpb_ref):
    @pl.when(pl.program_id(1) == 0)
    def _():
        wgb_ref[...] = wg_ref[...].astype(BF16)
        wpb_ref[...] = wp_ref[...].astype(BF16)

    gate = _sigmoid(jnp.dot(hn_ref[...], wgb_ref[...], preferred_element_type=F32))
    pe = jnp.dot(p_ref[...].astype(BF16), wpb_ref[...], preferred_element_type=F32)
    o_ref[...] = h_ref[...] + gate * pe


def ple(hn, p, wg, wp, layer, h, tm_pref=1024, tn_pref=512):
    t, d = h.shape
    pd = p.shape[2]
    tm, tn = _pick(t, tm_pref), _pick(d, tn_pref)
    return pl.pallas_call(
        _ple_kernel,
        grid=(d // tn, t // tm),
        in_specs=[
            pl.BlockSpec((tm, d), lambda j, i: (i, 0)),
            pl.BlockSpec((None, tm, pd), lambda j, i: (layer, i, 0)),
            pl.BlockSpec((None, d, tn), lambda j, i: (layer, 0, j)),
            pl.BlockSpec((None, pd, tn), lambda j, i: (layer, 0, j)),
            pl.BlockSpec((tm, tn), lambda j, i: (i, j)),
        ],
        out_specs=pl.BlockSpec((tm, tn), lambda j, i: (i, j)),
        out_shape=jax.ShapeDtypeStruct((t, d), F32),
        scratch_shapes=[pltpu.VMEM((d, tn), BF16), pltpu.VMEM((pd, tn), BF16)],
        compiler_params=_cparams("arbitrary", "arbitrary"),
        name="ple",
    )(hn, p, wg, wp, h)


def t5_bucket(dist):
    n = jnp.maximum(jnp.asarray(dist, jnp.int32), 0)
    exact = N_BUCKETS // 2
    nf = jnp.maximum(n, 1).astype(F32)
    large = exact + (jnp.log(nf / exact) / math.log(MAX_DIST / exact) * (N_BUCKETS - exact)).astype(jnp.int32)
    return jnp.where(n < exact, n, jnp.minimum(large, N_BUCKETS - 1))


def _bias_table_kernel(idx_ref, tab_ref, o_ref, *, head0):
    h = head0 + pl.program_id(0)
    idx = idx_ref[0]
    val = jnp.full(idx.shape, tab_ref[0, h], F32)
    for b in range(1, N_BUCKETS):
        val = jnp.where(idx == b, tab_ref[b, h], val)
    o_ref[0, 0] = val


def bias_table(bucket_idx, rel_bias, head0, n_heads):
    n, r, c = bucket_idx.shape
    return pl.pallas_call(
        functools.partial(_bias_table_kernel, head0=head0),
        grid=(n_heads, n),
        in_specs=[
            pl.BlockSpec((1, r, c), lambda h, i: (i, 0, 0)),
            pl.BlockSpec(memory_space=pltpu.SMEM),
        ],
        out_specs=pl.BlockSpec((1, 1, r, c), lambda h, i: (h, i, 0, 0)),
        out_shape=jax.ShapeDtypeStruct((n_heads, n, r, c), F32),
        compiler_params=_cparams("parallel", "parallel"),
        name="bias_table",
    )(bucket_idx, rel_bias.astype(F32))


def tile_bucket_idx(n_rel):
    a = np.arange(QBLK)[None, :, None]
    c = np.arange(QBLK)[None, None, :]
    r = np.arange(n_rel)[:, None, None]
    return t5_bucket(QBLK * r + a - c)


def cmp_bucket_idx(s, n_cmp_pad):
    t = np.arange(s).reshape(s // QBLK, QBLK, 1)
    cend = (np.arange(n_cmp_pad) * CMP_STRIDE + CMP_LEN - 1)[None, None, :]
    return t5_bucket(t - cend)


def _compress_kernel(t_ref, pos_ref, w1_ref, w2_ref, o_ref, tf_ref, *, n_pad):
    s = t_ref.shape[1]
    half = CMP_LEN // 2
    tf_ref[0:s, :] = t_ref[0].astype(F32)
    tf_ref[s:s + half, :] = jnp.zeros((half, A_DH), F32)
    hid = jnp.zeros((n_pad, w1_ref.shape[2]), F32)
    for l in range(CMP_LEN):
        rows = tf_ref[pl.ds(l, n_pad, stride=CMP_STRIDE), :] + pos_ref[0, l:l + 1, :]
        hid = hid + jnp.dot(rows.astype(BF16), w1_ref[0, l * A_DH:(l + 1) * A_DH, :], preferred_element_type=F32)
    act = jax.nn.gelu(hid)
    out = jnp.dot(act.astype(BF16), w2_ref[0], preferred_element_type=F32)
    valid = lax.broadcasted_iota(jnp.int32, out.shape, 0) < n_pad - 1
    o_ref[0, 0, 0] = jnp.where(valid, out, 0.0).astype(o_ref.dtype)


def compress(ze, cmp_pos, cmp_w1, cmp_w2):
    b, s, _ = ze.shape
    n_pad = s // CMP_STRIDE
    hid = cmp_w1.shape[2]
    return pl.pallas_call(
        functools.partial(_compress_kernel, n_pad=n_pad),
        grid=(2, b, A_KV),
        in_specs=[
            pl.BlockSpec((1, s, A_DH), lambda w, bi, k: (bi, 0, ZE_KC + 2 * w + k)),
            pl.BlockSpec((1, CMP_LEN, A_DH), lambda w, bi, k: (w, 0, 0)),
            pl.BlockSpec((1, CMP_LEN * A_DH, hid), lambda w, bi, k: (w, 0, 0)),
            pl.BlockSpec((1, hid, A_DH), lambda w, bi, k: (w, 0, 0)),
        ],
        out_specs=pl.BlockSpec((1, 1, 1, n_pad, A_DH), lambda w, bi, k: (w, bi, k, 0, 0)),
        out_shape=jax.ShapeDtypeStruct((2, b, A_KV, n_pad, A_DH), BF16),
        scratch_shapes=[pltpu.VMEM((s + CMP_LEN // 2, A_DH), F32)],
        compiler_params=_cparams("parallel", "parallel", "parallel"),
        name="nsa_compress",
    )(ze, cmp_pos.astype(F32), cmp_w1.astype(BF16), cmp_w2.astype(BF16))


def _masked_exp(s, mask):
    sm = jnp.where(mask, s, NEG)
    m = jnp.max(sm, axis=-1, keepdims=True)
    e = jnp.where(mask, jnp.exp(sm - m), 0.0)
    return e, 1.0 / jnp.maximum(jnp.sum(e, axis=-1, keepdims=True), 1e-30)


def _nsa_kernel(q_ref, kcm_ref, vcm_ref, ks_ref, vs_ref, kw_ref, vw_ref, g_ref, tw_ref, tc_ref, o_ref,
                qs_ref, sel_ref, m_ref, l_ref, acc_ref):
    i = pl.program_id(2)
    tq = QBLK
    n_cmp = kcm_ref.shape[3]
    n_slc = ks_ref.shape[1] // SLC_BLK
    nt = (((1,), (1,)), ((), ()))
    heads = [slice(g * tq, (g + 1) * tq) for g in range(A_GRP)]

    q = q_ref[0]
    for g in range(A_GRP):
        qs_ref[heads[g], :] = (q[:, g * A_DH:(g + 1) * A_DH].astype(F32) * (A_DH ** -0.5)).astype(BF16)
    t_col = i * tq + lax.broadcasted_iota(jnp.int32, (tq, 1), 0)

    n_row = lax.broadcasted_iota(jnp.int32, (tq, n_cmp), 1)
    mask_c = (t_col >= n_row * CMP_STRIDE + (CMP_LEN - 1)) & (n_row < n_cmp - 1)
    s_c = lax.dot_general(qs_ref[...], kcm_ref[0, 0, 0], nt, preferred_element_type=F32)
    e_c, r_c = zip(*[_masked_exp(s_c[heads[g]] + tc_ref[0, g, 0], mask_c) for g in range(A_GRP)])
    o_c = jnp.dot(jnp.concatenate(e_c, axis=0).astype(BF16), vcm_ref[0, 0, 0], preferred_element_type=F32)
    o_c = o_c * jnp.concatenate(r_c, axis=0)
    p_sum = (e_c[0] * r_c[0] + e_c[1] * r_c[1]) + (e_c[2] * r_c[2] + e_c[3] * r_c[3])
    ov_n = lax.broadcasted_iota(jnp.int32, (n_cmp, LANES), 0) * CMP_STRIDE
    ov_j = lax.broadcasted_iota(jnp.int32, (n_cmp, LANES), 1) * SLC_BLK
    overlap = ((ov_n < ov_j + SLC_BLK) & (ov_n + CMP_LEN > ov_j)).astype(F32)
    imp = jnp.dot(p_sum, overlap, preferred_element_type=F32, precision=lax.Precision.HIGHEST)

    imp_t = imp.T
    jb = lax.broadcasted_iota(jnp.int32, (LANES, tq), 0)
    t_row = i * tq + lax.broadcasted_iota(jnp.int32, (LANES, tq), 1)
    cur = t_row // SLC_BLK
    imp_t = jnp.where(jb * SLC_BLK > t_row, -BIG, imp_t)
    imp_t = jnp.where((jb == 0) | (jb == cur) | (jb == cur - 1), BIG, imp_t)
    sub = 8
    slabs = [imp_t[r * sub:(r + 1) * sub] for r in range(n_slc // sub)]
    jb_slab = lax.broadcasted_iota(jnp.int32, (sub, tq), 0)
    ranks = [jnp.zeros((sub, tq), F32) for _ in slabs]
    for j2 in range(n_slc):
        other = imp_t[j2:j2 + 1, :]
        for r, v in enumerate(slabs):
            if r * sub > j2:
                ahead = other >= v
            elif r * sub + sub - 1 <= j2:
                ahead = other > v
            else:
                ahead = (other > v) | ((other == v) & (jb_slab + r * sub > j2))
            ranks[r] = ranks[r] + jnp.where(ahead, 1.0, 0.0)
    sel_t = [jnp.where(rk < float(min(SLC_TOPK, n_slc)), 1.0, 0.0) for rk in ranks]
    sel_t.append(jnp.zeros((LANES - n_slc, tq), F32))
    sel_ref[...] = jnp.concatenate(sel_t, axis=0).T.astype(BF16)

    n_w = WIN_A // tq + 1
    jw0 = jnp.maximum(i - WIN_A // tq, 0)
    w_start = pl.multiple_of(jw0 * tq, tq)
    dist_w = t_col - (w_start + lax.broadcasted_iota(jnp.int32, (tq, n_w * tq), 1))
    mask_w = (dist_w >= 0) & (dist_w < WIN_A)
    rel_w = [jnp.clip(i - jw0 - cb, 0, FAR_REL) for cb in range(n_w)]
    s_w = lax.dot_general(qs_ref[...], kw_ref[0, pl.ds(w_start, n_w * tq), :], nt, preferred_element_type=F32)
    e_w, r_w = zip(*[_masked_exp(s_w[heads[g]] + jnp.concatenate([tw_ref[0, r, g] for r in rel_w], axis=-1), mask_w)
                     for g in range(A_GRP)])
    o_w = jnp.dot(jnp.concatenate(e_w, axis=0).astype(BF16), vw_ref[0, pl.ds(w_start, n_w * tq), :],
                  preferred_element_type=F32)
    o_w = o_w * jnp.concatenate(r_w, axis=0)

    ck = 4 * tq
    m_ref[...] = jnp.full(m_ref.shape, NEG, F32)
    l_ref[...] = jnp.zeros(l_ref.shape, F32)
    acc_ref[...] = jnp.zeros(acc_ref.shape, F32)
    blk_of_key = (lax.broadcasted_iota(jnp.int32, (LANES, ck), 0)
                  - lax.broadcasted_iota(jnp.int32, (LANES, ck), 1) // SLC_BLK)
    key_iota = lax.broadcasted_iota(jnp.int32, (tq, ck), 1)

    def chunk(c, carry):
        start = pl.multiple_of(c * ck, ck)
        expand = jnp.where(blk_of_key == (ck // SLC_BLK) * c, 1.0, 0.0).astype(BF16)
        chosen = jnp.dot(sel_ref[...], expand, preferred_element_type=F32)
        mask = jnp.where(start + key_iota <= t_col, chosen, 0.0) > 0.5
        rels = [jnp.clip(i - (ck // tq) * c - cb, 0, FAR_REL) for cb in range(ck // tq)]
        s_s = lax.dot_general(qs_ref[...], ks_ref[0, pl.ds(start, ck), :], nt, preferred_element_type=F32)
        ps = []
        for g in range(A_GRP):
            s_g = s_s[heads[g]] + jnp.concatenate([tw_ref[0, r, g] for r in rels], axis=-1)
            s_g = jnp.where(mask, s_g, NEG)
            m_old = m_ref[heads[g], :]
            m_new = jnp.maximum(m_old, jnp.max(s_g, axis=-1, keepdims=True))
            p = jnp.exp(s_g - m_new)
            alpha = jnp.exp(m_old - m_new)
            p_lanes = p[:, 0:tq]
            for cb in range(1, ck // tq):
                p_lanes = p_lanes + p[:, cb * tq:(cb + 1) * tq]
            l_ref[heads[g], :] = alpha * l_ref[heads[g], :] + p_lanes
            acc_ref[heads[g], :] = alpha * acc_ref[heads[g], :]
            m_ref[heads[g], :] = m_new
            ps.append(p.astype(BF16))
        acc_ref[...] += jnp.dot(jnp.concatenate(ps, axis=0), vs_ref[0, pl.ds(start, ck), :],
                                preferred_element_type=F32)
        return carry

    lax.fori_loop(0, (i * tq) // ck + 1, chunk, 0)

    gate = _sigmoid(g_ref[0].astype(F32))
    o_s = acc_ref[...] / jnp.maximum(jnp.sum(l_ref[...], axis=-1, keepdims=True), 1e-30)
    outs = []
    for g in range(A_GRP):
        outs.append(gate[:, g:g + 1] * o_c[heads[g]]
                    + gate[:, A_GRP + g:A_GRP + g + 1] * o_s[heads[g]]
                    + gate[:, 2 * A_GRP + g:2 * A_GRP + g + 1] * o_w[heads[g]])
    o_ref[0] = jnp.concatenate(outs, axis=-1).astype(o_ref.dtype)


def nsa_attention(ze, kvcm, tw, tc):
    b, s, _ = ze.shape
    n_cmp = kvcm.shape[3]

    def kv_spec(col):
        return pl.BlockSpec((1, s, A_DH), lambda bi, k, i: (bi, 0, col + k))

    return pl.pallas_call(
        _nsa_kernel,
        grid=(b, A_KV, s // QBLK),
        in_specs=[
            pl.BlockSpec((1, QBLK, A_GRP * A_DH), lambda bi, k, i: (bi, i, k)),
            pl.BlockSpec((1, 1, 1, n_cmp, A_DH), lambda bi, k, i: (0, bi, k, 0, 0)),
            pl.BlockSpec((1, 1, 1, n_cmp, A_DH), lambda bi, k, i: (1, bi, k, 0, 0)),
            kv_spec(ZE_KS), kv_spec(ZE_VS), kv_spec(ZE_KW), kv_spec(ZE_VW),
            pl.BlockSpec((1, QBLK, LANES), lambda bi, k, i: (bi, i, ZE_GATE + k)),
            pl.BlockSpec((1, FAR_REL + 1, A_GRP, QBLK, QBLK), lambda bi, k, i: (k, 0, 0, 0, 0)),
            pl.BlockSpec((1, A_GRP, 1, QBLK, n_cmp), lambda bi, k, i: (k, 0, i, 0, 0)),
        ],
        out_specs=pl.BlockSpec((1, QBLK, A_GRP * A_DH), lambda bi, k, i: (bi, i, k)),
        out_shape=jax.ShapeDtypeStruct((b, s, A_Q), BF16),
        scratch_shapes=[pltpu.VMEM((A_GRP * QBLK, A_DH), BF16), pltpu.VMEM((QBLK, LANES), BF16),
                        pltpu.VMEM((A_GRP * QBLK, 1), F32), pltpu.VMEM((A_GRP * QBLK, LANES), F32),
                        pltpu.VMEM((A_GRP * QBLK, A_DH), F32)],
        compiler_params=_cparams("parallel", "parallel", "arbitrary"),
        name="nsa_attention",
    )(ze, kvcm, kvcm, ze, ze, ze, ze, ze, tw, tc)


def _conv_kernel(gb_ref, gc_ref, hc_ref, gcp_ref, hcp_ref, w_ref, o_ref):
    i = pl.program_id(1)
    u = gc_ref[0].astype(F32) * hc_ref[0].astype(F32)
    prev = gcp_ref[0].astype(F32) * hcp_ref[0].astype(F32)
    prev = jnp.where(i > 0, prev, 0.0)
    n_prev = prev.shape[0]
    p1 = prev[n_prev - 1:n_prev, :]
    p2 = prev[n_prev - 2:n_prev - 1, :]
    row = lax.broadcasted_iota(jnp.int32, u.shape, 0)
    u1 = jnp.where(row == 0, p1, pltpu.roll(u, 1, 0))
    u2 = jnp.where(row == 0, p2, jnp.where(row == 1, p1, pltpu.roll(u, 2, 0)))
    w = w_ref[...]
    y = w[0:1, :] * u2 + w[1:2, :] * u1 + w[2:3, :] * u
    o_ref[0] = (gb_ref[0].astype(F32) * y).astype(o_ref.dtype)


def short_conv(ze, conv_w, ts_pref=512, halo=16):
    b, s, _ = ze.shape
    ts = _pick(s, ts_pref)
    cw = 4 * LANES
    nc = CONV_DIM // cw

    def cur(col):
        return pl.BlockSpec((1, ts, cw), lambda bi, i, c: (bi, i, col * LANES // cw + c))

    def prev(col):
        return pl.BlockSpec((1, halo, cw), lambda bi, i, c: (bi, jnp.maximum(i * (ts // halo) - 1, 0), col * LANES // cw + c))

    return pl.pallas_call(
        _conv_kernel,
        grid=(b, s // ts, nc),
        in_specs=[cur(ZE_GB), cur(ZE_GC), cur(ZE_HC), prev(ZE_GC), prev(ZE_HC),
                  pl.BlockSpec((CONV_K, cw), lambda bi, i, c: (0, c))],
        out_specs=pl.BlockSpec((1, ts, cw), lambda bi, i, c: (bi, i, c)),
        out_shape=jax.ShapeDtypeStruct((b, s, CONV_DIM), BF16),
        compiler_params=_cparams("parallel", "parallel", "parallel"),
        name="short_conv",
    )(ze, ze, ze, ze, ze, conv_w.astype(F32))


def _swa_kernel(q_ref, kp_ref, kc_ref, vp_ref, vc_ref, sink_ref, tb_ref, o_ref):
    i = pl.program_id(1)
    tq = QBLK
    nt = (((1,), (1,)), ((), ()))
    q = q_ref[0]
    a_row = lax.broadcasted_iota(jnp.int32, (tq, 2 * tq), 0)
    c_col = lax.broadcasted_iota(jnp.int32, (tq, 2 * tq), 1)
    dist = tq + a_row - c_col
    mask = ((dist >= 0) & (dist < WIN_C) & ((c_col >= tq) | (i > 0)))[None]
    outs = []
    for kv in range(C_KV):
        cols = slice(kv * C_DH, (kv + 1) * C_DH)
        k = jnp.concatenate([kp_ref[0][:, cols], kc_ref[0][:, cols]], axis=0)
        v = jnp.concatenate([vp_ref[0][:, cols], vc_ref[0][:, cols]], axis=0)
        qs = jnp.concatenate([q[:, (kv * C_GRP + g) * C_DH:(kv * C_GRP + g + 1) * C_DH] for g in range(C_GRP)], axis=0)
        qs = (qs.astype(F32) * (C_DH ** -0.5)).astype(BF16)
        s = lax.dot_general(qs, k, nt, preferred_element_type=F32).reshape(C_GRP, tq, 2 * tq)
        bias = jnp.concatenate([tb_ref[kv * C_GRP:(kv + 1) * C_GRP, 1], tb_ref[kv * C_GRP:(kv + 1) * C_GRP, 0]], axis=-1)
        sm = jnp.where(mask, s + bias, NEG)
        sink = jnp.concatenate([jnp.full((1, tq, 1), sink_ref[kv * C_GRP + g], F32) for g in range(C_GRP)], axis=0)
        m = jnp.maximum(jnp.max(sm, axis=-1, keepdims=True), sink)
        e = jnp.exp(sm - m)
        r = 1.0 / (jnp.sum(e, axis=-1, keepdims=True) + jnp.exp(sink - m))
        o = jnp.dot(e.reshape(C_GRP * tq, 2 * tq).astype(BF16), v, preferred_element_type=F32)
        o = o * r.reshape(C_GRP * tq, 1)
        outs.extend(o[g * tq:(g + 1) * tq] for g in range(C_GRP))
    o_ref[0] = jnp.concatenate(outs, axis=-1).astype(o_ref.dtype)


def swa_attention(zo, sinks, tb):
    b, s, _ = zo.shape
    kcol = C_Q // C_KVW
    prev = lambda bi, i: jnp.maximum(i - 1, 0)
    return pl.pallas_call(
        _swa_kernel,
        grid=(b, s // QBLK),
        in_specs=[
            pl.BlockSpec((1, QBLK, C_Q), lambda bi, i: (bi, i, 0)),
            pl.BlockSpec((1, QBLK, C_KVW), lambda bi, i: (bi, prev(bi, i), kcol)),
            pl.BlockSpec((1, QBLK, C_KVW), lambda bi, i: (bi, i, kcol)),
            pl.BlockSpec((1, QBLK, C_KVW), lambda bi, i: (bi, prev(bi, i), kcol + 1)),
            pl.BlockSpec((1, QBLK, C_KVW), lambda bi, i: (bi, i, kcol + 1)),
            pl.BlockSpec(memory_space=pltpu.SMEM),
            pl.BlockSpec((C_HEADS, 2, QBLK, QBLK), lambda bi, i: (0, 0, 0, 0)),
        ],
        out_specs=pl.BlockSpec((1, QBLK, C_Q), lambda bi, i: (bi, i, 0)),
        out_shape=jax.ShapeDtypeStruct((b, s, C_Q), BF16),
        compiler_params=_cparams("parallel", "parallel"),
        name="swa_attention",
    )(zo, zo, zo, zo, zo, sinks.astype(F32), tb)


def _router_kernel(h_ref, g_ref, wr_ref, xn_ref, idx_ref, wt_ref, *, n_experts):
    x = h_ref[...]
    xn = x * lax.rsqrt(jnp.mean(x * x, axis=-1, keepdims=True) + EPS) * g_ref[...]
    xn_ref[...] = xn
    logits = jnp.dot(xn, wr_ref[...], preferred_element_type=F32, precision=lax.Precision.HIGHEST)
    lane = lax.broadcasted_iota(jnp.int32, logits.shape, 1)
    lg = jnp.where(lane < n_experts, logits, NEG)
    m1 = jnp.max(lg, axis=-1, keepdims=True)
    i1 = jnp.min(jnp.where(lg == m1, lane, LANES), axis=-1, keepdims=True)
    lg2 = jnp.where(lane == i1, NEG, lg)
    m2 = jnp.max(lg2, axis=-1, keepdims=True)
    i2 = jnp.min(jnp.where(lg2 == m2, lane, LANES), axis=-1, keepdims=True)
    e2 = jnp.exp(m2 - m1)
    idx_ref[...] = jnp.where(lane == 0, i1, jnp.where(lane == 1, i2, 0))
    wt_ref[...] = jnp.where(lane == 0, 1.0 / (1.0 + e2), jnp.where(lane == 1, e2 / (1.0 + e2), 0.0))


def router(h, g, w_router):
    t, d = h.shape
    e = w_router.shape[1]
    tm = _pick(t, 512)
    wr = jnp.zeros((d, LANES), F32).at[:, :e].set(w_router.astype(F32))
    row = lambda i: (i, 0)
    return pl.pallas_call(
        functools.partial(_router_kernel, n_experts=e),
        grid=(t // tm,),
        in_specs=[pl.BlockSpec((tm, d), row), pl.BlockSpec((1, d), lambda i: (0, 0)),
                  pl.BlockSpec((d, LANES), lambda i: (0, 0))],
        out_specs=[pl.BlockSpec((tm, d), row), pl.BlockSpec((tm, LANES), row), pl.BlockSpec((tm, LANES), row)],
        out_shape=[jax.ShapeDtypeStruct((t, d), F32), jax.ShapeDtypeStruct((t, LANES), jnp.int32),
                   jax.ShapeDtypeStruct((t, LANES), F32)],
        compiler_params=_cparams("parallel"),
        name="moe_router",
    )(h, g.reshape(1, d).astype(F32), wr)


def _row_copy(src_hbm, dst_ref, sem, src_row, dst_row):
    return pltpu.make_async_copy(src_hbm.at[pl.ds(src_row, 1), :], dst_ref.at[pl.ds(dst_row, 1), :], sem)


def _gather_kernel(tok_ref, x_hbm, o_ref, buf_ref, sem):
    rows = o_ref.shape[0]

    def start(r, c):
        _row_copy(x_hbm, buf_ref, sem, tok_ref[0, 0, r], r).start()
        return c

    def wait(r, c):
        _row_copy(x_hbm, buf_ref, sem, 0, r).wait()
        return c

    lax.fori_loop(0, rows, start, 0)
    lax.fori_loop(0, rows, wait, 0)
    o_ref[...] = buf_ref[...].astype(o_ref.dtype)


def gather_rows(x, row_tok, tm, out_dtype):
    n_rows = row_tok.shape[0]
    d = x.shape[1]
    return pl.pallas_call(
        _gather_kernel,
        grid=(n_rows // tm,),
        in_specs=[pl.BlockSpec((1, 1, tm), lambda i: (i, 0, 0), memory_space=pltpu.SMEM),
                  pl.BlockSpec(memory_space=pl.ANY)],
        out_specs=pl.BlockSpec((tm, d), lambda i: (i, 0)),
        out_shape=jax.ShapeDtypeStruct((n_rows, d), out_dtype),
        scratch_shapes=[pltpu.VMEM((tm, d), x.dtype), pltpu.SemaphoreType.DMA(())],
        compiler_params=_cparams("arbitrary"),
        name="moe_gather",
    )(row_tok.reshape(n_rows // tm, 1, tm), x)


def _combine_kernel(p0_ref, p1_ref, y_hbm, h_ref, o_ref, a_ref, b_ref, sem):
    rows = o_ref.shape[0]

    def start(r, c):
        _row_copy(y_hbm, a_ref, sem.at[0], p0_ref[0, 0, r], r).start()
        _row_copy(y_hbm, b_ref, sem.at[1], p1_ref[0, 0, r], r).start()
        return c

    def wait(r, c):
        _row_copy(y_hbm, a_ref, sem.at[0], 0, r).wait()
        _row_copy(y_hbm, b_ref, sem.at[1], 0, r).wait()
        return c

    lax.fori_loop(0, rows, start, 0)
    lax.fori_loop(0, rows, wait, 0)
    o_ref[...] = h_ref[...] + a_ref[...] + b_ref[...]


def combine_rows(y_rows, pos0, pos1, h, tm):
    t, d = h.shape
    idx_spec = pl.BlockSpec((1, 1, tm), lambda i: (i, 0, 0), memory_space=pltpu.SMEM)
    return pl.pallas_call(
        _combine_kernel,
        grid=(t // tm,),
        in_specs=[idx_spec, idx_spec, pl.BlockSpec(memory_space=pl.ANY), pl.BlockSpec((tm, d), lambda i: (i, 0))],
        out_specs=pl.BlockSpec((tm, d), lambda i: (i, 0)),
        out_shape=jax.ShapeDtypeStruct((t, d), F32),
        scratch_shapes=[pltpu.VMEM((tm, d), F32), pltpu.VMEM((tm, d), F32), pltpu.SemaphoreType.DMA((2,))],
        compiler_params=_cparams("arbitrary"),
        name="moe_combine",
    )(pos0.reshape(t // tm, 1, tm), pos1.reshape(t // tm, 1, tm), y_rows, h)


def moe_layer(h, g, w_router, wg, wu, wd, layer, tm_pref=512):
    t, d = h.shape
    n_exp = w_router.shape[1]
    tm = _pick(t, tm_pref)
    xn, idx, wt = router(h, g, w_router)
    e_flat = idx[:, :TOP_K].reshape(-1)
    w_flat = wt[:, :TOP_K].reshape(-1)
    onehot = (e_flat[:, None] == jnp.arange(n_exp)[None, :]).astype(jnp.int32)
    csum = jnp.cumsum(onehot, axis=0)
    rank = jnp.sum((csum - onehot) * onehot, axis=1)
    counts = csum[-1]
    padded = (counts + tm - 1) // tm * tm
    pad_end = jnp.cumsum(padded)
    dest = (pad_end - padded)[e_flat] + rank
    n_rows = t * TOP_K + n_exp * tm
    row_tok = jnp.zeros((n_rows,), jnp.int32).at[dest].set(jnp.arange(t * TOP_K, dtype=jnp.int32) // TOP_K)
    row_w = jnp.zeros((n_rows,), F32).at[dest].set(w_flat)
    n_blk = n_rows // tm
    blk = jnp.arange(n_blk)
    n_used = (pad_end[-1:] // tm).astype(jnp.int32)
    blk_expert = jnp.minimum(jnp.searchsorted(pad_end, blk * tm, side='right'), n_exp - 1).astype(jnp.int32)
    blk_expert = jnp.where(blk < n_used[0], blk_expert, blk_expert[n_used[0] - 1])
    first = ((blk == 0) | (blk_expert != jnp.roll(blk_expert, 1))).astype(jnp.int32)
    groups = (layer * n_exp + blk_expert, first, n_used)

    xs = gather_rows(xn, row_tok, tm, BF16)
    hid = grouped_glu(xs, wg, wu, groups, tm)
    y_rows = grouped_matmul([hid], wd, groups, F32, tm, 512, scale=row_w.reshape(n_rows, 1))
    pos = dest.reshape(t, TOP_K).astype(jnp.int32)
    return combine_rows(y_rows, pos[:, 0], pos[:, 1], h, tm)


def _even_in_proj_weight(w_in):
    d = w_in.shape[0]
    g0 = A_Q + 6 * A_KVW
    gates = w_in[:, g0:g0 + 3 * A_HEADS].reshape(d, 3, A_KV, A_GRP)
    blocks = [w_in[:, :g0], w_in[:, g0 + 3 * A_HEADS:]]
    for k in range(A_KV):
        gk = gates[:, :, k, :].reshape(d, 3 * A_GRP)
        blocks.append(jnp.pad(gk, ((0, 0), (0, LANES - 3 * A_GRP))))
    w = jnp.concatenate(blocks, axis=1)
    return jnp.pad(w, ((0, 0), (0, ZE_WIDTH - w.shape[1])))


def kernel(x, p, rel_bias, norm_mix, norm_ffn, norm_ple, norm_final, w_in_e, cmp_pos, cmp_w1, cmp_w2, conv_w,
           w_out_e, w_gate_d, w_up_d, w_down_d, w_qkv_o, sinks, w_out_o, w_router, w_gate_m, w_up_m, w_down_m,
           w_ple, w_ple_gate):
    b, s, d = x.shape
    t = b * s
    depth = norm_mix.shape[0]
    n_cmp_pad = s // CMP_STRIDE
    tm = _pick(t, 1024)
    tm_down = _pick(t, 512)

    tw = bias_table(tile_bucket_idx(FAR_REL + 1), rel_bias, 0, A_HEADS)
    tw = tw.reshape(A_KV, A_GRP, FAR_REL + 1, QBLK, QBLK).transpose(0, 2, 1, 3, 4)
    tc = bias_table(cmp_bucket_idx(s, n_cmp_pad), rel_bias, 0, A_HEADS)
    tc = tc.reshape(A_KV, A_GRP, s // QBLK, QBLK, n_cmp_pad)
    tb = bias_table(tile_bucket_idx(2), rel_bias, A_HEADS, C_HEADS)

    w_in_relaid = jnp.stack([_even_in_proj_weight(w_in_e[j]) for j in range(w_in_e.shape[0])])
    w_gate_m = w_gate_m.reshape((-1,) + w_gate_m.shape[2:])
    w_up_m = w_up_m.reshape((-1,) + w_up_m.shape[2:])
    w_down_m = w_down_m.reshape((-1,) + w_down_m.shape[2:])
    p = p.reshape(depth, t, -1)

    h = x.reshape(t, d).astype(F32)
    for i in range(depth):
        j = i // 2
        layer = _dense_groups(t // tm, j)
        hn = rmsnorm(h, norm_mix[i], BF16)
        if i % 2 == 0:
            ze = grouped_matmul([hn], w_in_relaid, layer, BF16, tm, 1024).reshape(b, s, ZE_WIDTH)
            kvcm = compress(ze, cmp_pos[j], cmp_w1[j], cmp_w2[j])
            att = nsa_attention(ze, kvcm, tw, tc).reshape(t, A_Q)
            cnv = short_conv(ze, conv_w[j]).reshape(t, CONV_DIM)
            h = grouped_matmul([att, cnv], w_out_e, layer, F32, tm, 512, res=h)
            hn = rmsnorm(h, norm_ffn[i], BF16)
            hid = grouped_glu(hn, w_gate_d, w_up_d, layer, tm)
            h = grouped_matmul([hid], w_down_d, _dense_groups(t // tm_down, j), F32, tm_down, 512, res=h)
        else:
            zo = grouped_matmul([hn], w_qkv_o, layer, BF16, tm, 1280).reshape(b, s, -1)
            att = swa_attention(zo, sinks[j], tb).reshape(t, C_Q)
            h = grouped_matmul([att], w_out_o, layer, F32, tm, 512, res=h)
            h = moe_layer(h, norm_ffn[i], w_router[j], w_gate_m, w_up_m, w_down_m, j)
        hn = rmsnorm(h, norm_ple[i], BF16)
        h = ple(hn, p, w_ple_gate, w_ple, i, h)
    return rmsnorm(h, norm_final, x.dtype).reshape(b, s, d)
```

```python
import functools
import math

import numpy as np
import jax
import jax.numpy as jnp
from jax import lax
from jax.experimental import pallas as pl
from jax.experimental.pallas import tpu as pltpu

F32 = jnp.float32
BF16 = jnp.bfloat16

LANES = 128
VMEM_LIMIT_BYTES = 56 * 1024 * 1024

EPS = 1e-6
A_HEADS = 8
A_KV = 2
A_GRP = A_HEADS // A_KV
A_DH = 128
A_Q = A_HEADS * A_DH
A_KVW = A_KV * A_DH
CMP_LEN = 32
CMP_STRIDE = 16
SLC_BLK = 64
SLC_TOPK = 16
WIN_A = 512
CONV_DIM = 1024
CONV_K = 3
C_HEADS = 32
C_KV = 4
C_GRP = C_HEADS // C_KV
C_DH = 64
C_Q = C_HEADS * C_DH
C_KVW = C_KV * C_DH
WIN_C = 128
N_BUCKETS = 32
MAX_DIST = 1024
TOP_K = 2

QBLK = 128
FAR_REL = 8
NEG = -1e30
BIG = 1e30

ZE_Q = 0
ZE_KC, ZE_VC, ZE_KS, ZE_VS, ZE_KW, ZE_VW = 8, 10, 12, 14, 16, 18
ZE_GB, ZE_GC, ZE_HC = 20, 28, 36
ZE_GATE = 44
ZE_WIDTH = 48 * LANES


def _cparams(*sem):
    return pltpu.CompilerParams(dimension_semantics=sem, vmem_limit_bytes=VMEM_LIMIT_BYTES)


def _sigmoid(v):
    return 1.0 / (1.0 + jnp.exp(-v))


def _pick(n, pref):
    t = min(n, pref)
    while n % t:
        t -= LANES if t > LANES else 8
    return t


def _rmsnorm_kernel(x_ref, g_ref, o_ref):
    x = x_ref[...]
    y = x * lax.rsqrt(jnp.mean(x * x, axis=-1, keepdims=True) + EPS)
    o_ref[...] = (y * g_ref[...]).astype(o_ref.dtype)


def rmsnorm(x, g, out_dtype):
    t, d = x.shape
    tm = _pick(t, 512)
    return pl.pallas_call(
        _rmsnorm_kernel,
        grid=(t // tm,),
        in_specs=[pl.BlockSpec((tm, d), lambda i: (i, 0)), pl.BlockSpec((1, d), lambda i: (0, 0))],
        out_specs=pl.BlockSpec((tm, d), lambda i: (i, 0)),
        out_shape=jax.ShapeDtypeStruct((t, d), out_dtype),
        compiler_params=_cparams("parallel"),
        name="rmsnorm",
    )(x, g.reshape(1, d).astype(F32))


def _dense_groups(n_blk, layer):
    grp = jnp.full((n_blk,), layer, jnp.int32)
    first = (jnp.arange(n_blk) == 0).astype(jnp.int32)
    return grp, first, jnp.full((1,), n_blk, jnp.int32)


def _gmm_kernel(grp_ref, first_ref, nu_ref, *refs, k_sizes, has_res):
    n_x = len(k_sizes)
    w_ref = refs[n_x]
    o_ref, wb_ref = refs[-2], refs[-1]
    i = pl.program_id(1)

    @pl.when(first_ref[i] == 1)
    def _():
        wb_ref[...] = w_ref[...].astype(BF16)

    @pl.when(i < nu_ref[0])
    def _():
        acc = None
        off = 0
        for x_ref, k in zip(refs[:n_x], k_sizes):
            part = jnp.dot(x_ref[...], wb_ref[off:off + k, :], preferred_element_type=F32)
            acc = part if acc is None else acc + part
            off += k
        if has_res:
            acc = refs[n_x + 1][...] + acc
        o_ref[...] = acc.astype(o_ref.dtype)

    @pl.when(i >= nu_ref[0])
    def _():
        o_ref[...] = jnp.zeros(o_ref.shape, o_ref.dtype)


def grouped_matmul(xs, w, groups, out_dtype, tm, tn_pref, res=None):
    t = xs[0].shape[0]
    kw, n = w.shape[1], w.shape[2]
    k_sizes = tuple(x.shape[1] for x in xs)
    tn = _pick(n, tn_pref)
    in_specs = [pl.BlockSpec((tm, k), lambda j, i, g, f, nu: (i, 0)) for k in k_sizes]
    in_specs.append(pl.BlockSpec((None, kw, tn), lambda j, i, g, f, nu: (g[i], 0, j)))
    args = list(xs) + [w]
    if res is not None:
        in_specs.append(pl.BlockSpec((tm, tn), lambda j, i, g, f, nu: (i, j)))
        args.append(res)
    grid_spec = pltpu.PrefetchScalarGridSpec(
        num_scalar_prefetch=3,
        grid=(n // tn, t // tm),
        in_specs=in_specs,
        out_specs=pl.BlockSpec((tm, tn), lambda j, i, g, f, nu: (i, j)),
        scratch_shapes=[pltpu.VMEM((kw, tn), BF16)],
    )
    return pl.pallas_call(
        functools.partial(_gmm_kernel, k_sizes=k_sizes, has_res=res is not None),
        grid_spec=grid_spec,
        out_shape=jax.ShapeDtypeStruct((t, n), out_dtype),
        compiler_params=_cparams("arbitrary", "arbitrary"),
        name="matmul",
    )(*groups, *args)


def _glu_kernel(grp_ref, first_ref, nu_ref, x_ref, wg_ref, wu_ref, o_ref, wgb_ref, wub_ref):
    i = pl.program_id(1)

    @pl.when(first_ref[i] == 1)
    def _():
        wgb_ref[...] = wg_ref[...].astype(BF16)
        wub_ref[...] = wu_ref[...].astype(BF16)

    @pl.when(i < nu_ref[0])
    def _():
        x = x_ref[...]
        g = jnp.dot(x, wgb_ref[...], preferred_element_type=F32)
        u = jnp.dot(x, wub_ref[...], preferred_element_type=F32)
        o_ref[...] = (g * _sigmoid(g) * u).astype(o_ref.dtype)

    @pl.when(i >= nu_ref[0])
    def _():
        o_ref[...] = jnp.zeros(o_ref.shape, o_ref.dtype)


def grouped_glu(x, wg, wu, groups, tm, tf_pref=512):
    t, d = x.shape
    f = wg.shape[2]
    tf = _pick(f, tf_pref)
    w_spec = pl.BlockSpec((None, d, tf), lambda j, i, g, fl, nu: (g[i], 0, j))
    grid_spec = pltpu.PrefetchScalarGridSpec(
        num_scalar_prefetch=3,
        grid=(f // tf, t // tm),
        in_specs=[pl.BlockSpec((tm, d), lambda j, i, g, fl, nu: (i, 0)), w_spec, w_spec],
        out_specs=pl.BlockSpec((tm, tf), lambda j, i, g, fl, nu: (i, j)),
        scratch_shapes=[pltpu.VMEM((d, tf), BF16), pltpu.VMEM((d, tf), BF16)],
    )
    return pl.pallas_call(
        _glu_kernel,
        grid_spec=grid_spec,
        out_shape=jax.ShapeDtypeStruct((t, f), BF16),
        compiler_params=_cparams("arbitrary", "arbitrary"),
        name="glu",
    )(*groups, x, wg, wu)


def _ple_kernel(hn_ref, p_ref, wg_ref, wp_ref, h_ref, o_ref, wgb_ref, wpb_ref):
    @pl.when(pl.program_id(1) == 0)
    def _():
        wgb_ref[...] = wg_ref[...].astype(BF16)
        wpb_ref[...] = wp_ref[...].astype(BF16)

    gate = _sigmoid(jnp.dot(hn_ref[...], wgb_ref[...], preferred_element_type=F32))
    pe = jnp.dot(p_ref[...].astype(BF16), wpb_ref[...], preferred_element_type=F32)
    o_ref[...] = h_ref[...] + gate * pe


def ple(hn, p, wg, wp, layer, h, tm_pref=1024, tn_pref=512):
    t, d = h.shape
    pd = p.shape[2]
    tm, tn = _pick(t, tm_pref), _pick(d, tn_pref)
    return pl.pallas_call(
        _ple_kernel,
        grid=(d // tn, t // tm),
        in_specs=[
            pl.BlockSpec((tm, d), lambda j, i: (i, 0)),
            pl.BlockSpec((None, tm, pd), lambda j, i: (layer, i, 0)),
            pl.BlockSpec((None, d, tn), lambda j, i: (layer, 0, j)),
            pl.BlockSpec((None, pd, tn), lambda j, i: (layer, 0, j)),
            pl.BlockSpec((tm, tn), lambda j, i: (i, j)),
        ],
        out_specs=pl.BlockSpec((tm, tn), lambda j, i: (i, j)),
        out_shape=jax.ShapeDtypeStruct((t, d), F32),
        scratch_shapes=[pltpu.VMEM((d, tn), BF16), pltpu.VMEM((pd, tn), BF16)],
        compiler_params=_cparams("arbitrary", "arbitrary"),
        name="ple",
    )(hn, p, wg, wp, h)


def t5_bucket(dist):
    n = jnp.maximum(jnp.asarray(dist, jnp.int32), 0)
    exact = N_BUCKETS // 2
    nf = jnp.maximum(n, 1).astype(F32)
    large = exact + (jnp.log(nf / exact) / math.log(MAX_DIST / exact) * (N_BUCKETS - exact)).astype(jnp.int32)
    return jnp.where(n < exact, n, jnp.minimum(large, N_BUCKETS - 1))


def _bias_table_kernel(idx_ref, tab_ref, o_ref, *, head0):
    h = head0 + pl.program_id(0)
    idx = idx_ref[0]
    val = jnp.full(idx.shape, tab_ref[0, h], F32)
    for b in range(1, N_BUCKETS):
        val = jnp.where(idx == b, tab_ref[b, h], val)
    o_ref[0, 0] = jnp.where(idx < 0, NEG, val)


def bias_table(bucket_idx, rel_bias, head0, n_heads):
    n, r, c = bucket_idx.shape
    return pl.pallas_call(
        functools.partial(_bias_table_kernel, head0=head0),
        grid=(n_heads, n),
        in_specs=[
            pl.BlockSpec((1, r, c), lambda h, i: (i, 0, 0)),
            pl.BlockSpec(memory_space=pltpu.SMEM),
        ],
        out_specs=pl.BlockSpec((1, 1, r, c), lambda h, i: (h, i, 0, 0)),
        out_shape=jax.ShapeDtypeStruct((n_heads, n, r, c), F32),
        compiler_params=_cparams("parallel", "parallel"),
        name="bias_table",
    )(bucket_idx, rel_bias.astype(F32))


def tile_bucket_idx(n_rel):
    a = np.arange(QBLK)[None, :, None]
    c = np.arange(QBLK)[None, None, :]
    r = np.arange(n_rel)[:, None, None]
    return t5_bucket(QBLK * r + a - c)


def cmp_bucket_idx(s, n_cmp_pad):
    t = np.arange(s).reshape(s // QBLK, QBLK, 1)
    cend = (np.arange(n_cmp_pad) * CMP_STRIDE + CMP_LEN - 1)[None, None, :]
    return t5_bucket(t - cend)


def _compress_kernel(t_ref, pos_ref, w1_ref, w2_ref, o_ref, tf_ref, *, n_pad):
    s = t_ref.shape[1]
    half = CMP_LEN // 2
    tf_ref[0:s, :] = t_ref[0].astype(F32)
    tf_ref[s:s + half, :] = jnp.zeros((half, A_DH), F32)
    hid = jnp.zeros((n_pad, w1_ref.shape[2]), F32)
    for l in range(CMP_LEN):
        rows = tf_ref[pl.ds(l, n_pad, stride=CMP_STRIDE), :] + pos_ref[0, l:l + 1, :]
        hid = hid + jnp.dot(rows.astype(BF16), w1_ref[0, l * A_DH:(l + 1) * A_DH, :], preferred_element_type=F32)
    act = jax.nn.gelu(hid)
    out = jnp.dot(act.astype(BF16), w2_ref[0], preferred_element_type=F32)
    valid = lax.broadcasted_iota(jnp.int32, out.shape, 0) < n_pad - 1
    o_ref[0, 0, 0] = jnp.where(valid, out, 0.0).astype(o_ref.dtype)


def compress(ze, cmp_pos, cmp_w1, cmp_w2):
    b, s, _ = ze.shape
    n_pad = s // CMP_STRIDE
    hid = cmp_w1.shape[2]
    return pl.pallas_call(
        functools.partial(_compress_kernel, n_pad=n_pad),
        grid=(2, b, A_KV),
        in_specs=[
            pl.BlockSpec((1, s, A_DH), lambda w, bi, k: (bi, 0, ZE_KC + 2 * w + k)),
            pl.BlockSpec((1, CMP_LEN, A_DH), lambda w, bi, k: (w, 0, 0)),
            pl.BlockSpec((1, CMP_LEN * A_DH, hid), lambda w, bi, k: (w, 0, 0)),
            pl.BlockSpec((1, hid, A_DH), lambda w, bi, k: (w, 0, 0)),
        ],
        out_specs=pl.BlockSpec((1, 1, 1, n_pad, A_DH), lambda w, bi, k: (w, bi, k, 0, 0)),
        out_shape=jax.ShapeDtypeStruct((2, b, A_KV, n_pad, A_DH), BF16),
        scratch_shapes=[pltpu.VMEM((s + CMP_LEN // 2, A_DH), F32)],
        compiler_params=_cparams("parallel", "parallel", "parallel"),
        name="nsa_compress",
    )(ze, cmp_pos.astype(F32), cmp_w1.astype(BF16), cmp_w2.astype(BF16))


def _masked_exp(s, mask):
    sm = jnp.where(mask, s, NEG)
    m = jnp.max(sm, axis=-1, keepdims=True)
    e = jnp.where(mask, jnp.exp(sm - m), 0.0)
    return e, 1.0 / jnp.maximum(jnp.sum(e, axis=-1, keepdims=True), 1e-30)


def _nsa_kernel(q_ref, kcm_ref, vcm_ref, ks_ref, vs_ref, kw_ref, vw_ref, g_ref, tw_ref, tc_ref, o_ref,
                qs_ref, sel_ref, m_ref, l_ref, acc_ref):
    i = pl.program_id(2)
    tq = QBLK
    n_cmp = kcm_ref.shape[3]
    n_slc = ks_ref.shape[1] // SLC_BLK
    nt = (((1,), (1,)), ((), ()))
    heads = [slice(g * tq, (g + 1) * tq) for g in range(A_GRP)]

    q = q_ref[0]
    for g in range(A_GRP):
        qs_ref[heads[g], :] = (q[:, g * A_DH:(g + 1) * A_DH].astype(F32) * (A_DH ** -0.5)).astype(BF16)
    t_col = i * tq + lax.broadcasted_iota(jnp.int32, (tq, 1), 0)

    n_row = lax.broadcasted_iota(jnp.int32, (tq, n_cmp), 1)
    mask_c = (t_col >= n_row * CMP_STRIDE + (CMP_LEN - 1)) & (n_row < n_cmp - 1)
    s_c = lax.dot_general(qs_ref[...], kcm_ref[0, 0, 0], nt, preferred_element_type=F32)
    e_c, r_c = zip(*[_masked_exp(s_c[heads[g]] + tc_ref[0, g, 0], mask_c) for g in range(A_GRP)])
    o_c = jnp.dot(jnp.concatenate(e_c, axis=0).astype(BF16), vcm_ref[0, 0, 0], preferred_element_type=F32)
    o_c = o_c * jnp.concatenate(r_c, axis=0)
    p_sum = (e_c[0] * r_c[0] + e_c[1] * r_c[1]) + (e_c[2] * r_c[2] + e_c[3] * r_c[3])
    ov_n = lax.broadcasted_iota(jnp.int32, (n_cmp, LANES), 0) * CMP_STRIDE
    ov_j = lax.broadcasted_iota(jnp.int32, (n_cmp, LANES), 1) * SLC_BLK
    overlap = ((ov_n < ov_j + SLC_BLK) & (ov_n + CMP_LEN > ov_j)).astype(F32)
    imp = jnp.dot(p_sum, overlap, preferred_element_type=F32, precision=lax.Precision.HIGHEST)

    imp_t = imp.T
    jb = lax.broadcasted_iota(jnp.int32, (LANES, tq), 0)
    t_row = i * tq + lax.broadcasted_iota(jnp.int32, (LANES, tq), 1)
    cur = t_row // SLC_BLK
    imp_t = jnp.where(jb * SLC_BLK > t_row, -BIG, imp_t)
    imp_t = jnp.where((jb == 0) | (jb == cur) | (jb == cur - 1), BIG, imp_t)
    sub = 8
    slabs = [imp_t[r * sub:(r + 1) * sub] for r in range(n_slc // sub)]
    jb_slab = lax.broadcasted_iota(jnp.int32, (sub, tq), 0)
    ranks = [jnp.zeros((sub, tq), F32) for _ in slabs]
    for j2 in range(n_slc):
        other = imp_t[j2:j2 + 1, :]
        for r, v in enumerate(slabs):
            if r * sub > j2:
                ahead = other >= v
            elif r * sub + sub - 1 <= j2:
                ahead = other > v
            else:
                ahead = (other > v) | ((other == v) & (jb_slab + r * sub > j2))
            ranks[r] = ranks[r] + jnp.where(ahead, 1.0, 0.0)
    sel_t = [jnp.where(rk < float(min(SLC_TOPK, n_slc)), 1.0, 0.0) for rk in ranks]
    sel_t.append(jnp.zeros((LANES - n_slc, tq), F32))
    sel_ref[...] = jnp.concatenate(sel_t, axis=0).T.astype(BF16)

    n_w = WIN_A // tq + 1
    jw0 = jnp.maximum(i - WIN_A // tq, 0)
    w_start = pl.multiple_of(jw0 * tq, tq)
    dist_w = t_col - (w_start + lax.broadcasted_iota(jnp.int32, (tq, n_w * tq), 1))
    mask_w = (dist_w >= 0) & (dist_w < WIN_A)
    rel_w = [jnp.clip(i - jw0 - cb, 0, FAR_REL) for cb in range(n_w)]
    s_w = lax.dot_general(qs_ref[...], kw_ref[0, pl.ds(w_start, n_w * tq), :], nt, preferred_element_type=F32)
    e_w, r_w = zip(*[_masked_exp(s_w[heads[g]] + jnp.concatenate([tw_ref[0, r, g] for r in rel_w], axis=-1), mask_w)
                     for g in range(A_GRP)])
    o_w = jnp.dot(jnp.concatenate(e_w, axis=0).astype(BF16), vw_ref[0, pl.ds(w_start, n_w * tq), :],
                  preferred_element_type=F32)
    o_w = o_w * jnp.concatenate(r_w, axis=0)

    ck = 4 * tq
    m_ref[...] = jnp.full(m_ref.shape, NEG, F32)
    l_ref[...] = jnp.zeros(l_ref.shape, F32)
    acc_ref[...] = jnp.zeros(acc_ref.shape, F32)
    blk_of_key = (lax.broadcasted_iota(jnp.int32, (LANES, ck), 0)
                  - lax.broadcasted_iota(jnp.int32, (LANES, ck), 1) // SLC_BLK)
    key_iota = lax.broadcasted_iota(jnp.int32, (tq, ck), 1)

    def chunk(c, carry):
        start = pl.multiple_of(c * ck, ck)
        expand = jnp.where(blk_of_key == (ck // SLC_BLK) * c, 1.0, 0.0).astype(BF16)
        chosen = jnp.dot(sel_ref[...], expand, preferred_element_type=F32)
        mask = jnp.where(start + key_iota <= t_col, chosen, 0.0) > 0.5
        rels = [jnp.clip(i - (ck // tq) * c - cb, 0, FAR_REL) for cb in range(ck // tq)]
        s_s = lax.dot_general(qs_ref[...], ks_ref[0, pl.ds(start, ck), :], nt, preferred_element_type=F32)
        ps = []
        for g in range(A_GRP):
            s_g = s_s[heads[g]] + jnp.concatenate([tw_ref[0, r, g] for r in rels], axis=-1)
            s_g = jnp.where(mask, s_g, NEG)
            m_old = m_ref[heads[g], :]
            m_new = jnp.maximum(m_old, jnp.max(s_g, axis=-1, keepdims=True))
            p = jnp.exp(s_g - m_new)
            alpha = jnp.exp(m_old - m_new)
            p_lanes = p[:, 0:tq]
            for cb in range(1, ck // tq):
                p_lanes = p_lanes + p[:, cb * tq:(cb + 1) * tq]
            l_ref[heads[g], :] = alpha * l_ref[heads[g], :] + p_lanes
            acc_ref[heads[g], :] = alpha * acc_ref[heads[g], :]
            m_ref[heads[g], :] = m_new
            ps.append(p.astype(BF16))
        acc_ref[...] += jnp.dot(jnp.concatenate(ps, axis=0), vs_ref[0, pl.ds(start, ck), :],
                                preferred_element_type=F32)
        return carry

    lax.fori_loop(0, (i * tq) // ck + 1, chunk, 0)

    gate = _sigmoid(g_ref[0].astype(F32))
    o_s = acc_ref[...] / jnp.maximum(jnp.sum(l_ref[...], axis=-1, keepdims=True), 1e-30)
    outs = []
    for g in range(A_GRP):
        outs.append(gate[:, g:g + 1] * o_c[heads[g]]
                    + gate[:, A_GRP + g:A_GRP + g + 1] * o_s[heads[g]]
                    + gate[:, 2 * A_GRP + g:2 * A_GRP + g + 1] * o_w[heads[g]])
    o_ref[0] = jnp.concatenate(outs, axis=-1).astype(o_ref.dtype)


def nsa_attention(ze, kvcm, tw, tc):
    b, s, _ = ze.shape
    n_cmp = kvcm.shape[3]

    def kv_spec(col):
        return pl.BlockSpec((1, s, A_DH), lambda bi, k, i: (bi, 0, col + k))

    return pl.pallas_call(
        _nsa_kernel,
        grid=(b, A_KV, s // QBLK),
        in_specs=[
            pl.BlockSpec((1, QBLK, A_GRP * A_DH), lambda bi, k, i: (bi, i, k)),
            pl.BlockSpec((1, 1, 1, n_cmp, A_DH), lambda bi, k, i: (0, bi, k, 0, 0)),
            pl.BlockSpec((1, 1, 1, n_cmp, A_DH), lambda bi, k, i: (1, bi, k, 0, 0)),
            kv_spec(ZE_KS), kv_spec(ZE_VS), kv_spec(ZE_KW), kv_spec(ZE_VW),
            pl.BlockSpec((1, QBLK, LANES), lambda bi, k, i: (bi, i, ZE_GATE + k)),
            pl.BlockSpec((1, FAR_REL + 1, A_GRP, QBLK, QBLK), lambda bi, k, i: (k, 0, 0, 0, 0)),
            pl.BlockSpec((1, A_GRP, 1, QBLK, n_cmp), lambda bi, k, i: (k, 0, i, 0, 0)),
        ],
        out_specs=pl.BlockSpec((1, QBLK, A_GRP * A_DH), lambda bi, k, i: (bi, i, k)),
        out_shape=jax.ShapeDtypeStruct((b, s, A_Q), BF16),
        scratch_shapes=[pltpu.VMEM((A_GRP * QBLK, A_DH), BF16), pltpu.VMEM((QBLK, LANES), BF16),
                        pltpu.VMEM((A_GRP * QBLK, 1), F32), pltpu.VMEM((A_GRP * QBLK, LANES), F32),
                        pltpu.VMEM((A_GRP * QBLK, A_DH), F32)],
        compiler_params=_cparams("parallel", "parallel", "arbitrary"),
        name="nsa_attention",
    )(ze, kvcm, kvcm, ze, ze, ze, ze, ze, tw, tc)


def _conv_kernel(gb_ref, gc_ref, hc_ref, gcp_ref, hcp_ref, w_ref, o_ref):
    i = pl.program_id(1)
    u = gc_ref[0].astype(F32) * hc_ref[0].astype(F32)
    prev = gcp_ref[0].astype(F32) * hcp_ref[0].astype(F32)
    prev = jnp.where(i > 0, prev, 0.0)
    n_prev = prev.shape[0]
    p1 = prev[n_prev - 1:n_prev, :]
    p2 = prev[n_prev - 2:n_prev - 1, :]
    row = lax.broadcasted_iota(jnp.int32, u.shape, 0)
    u1 = jnp.where(row == 0, p1, pltpu.roll(u, 1, 0))
    u2 = jnp.where(row == 0, p2, jnp.where(row == 1, p1, pltpu.roll(u, 2, 0)))
    w = w_ref[...]
    y = w[0:1, :] * u2 + w[1:2, :] * u1 + w[2:3, :] * u
    o_ref[0] = (gb_ref[0].astype(F32) * y).astype(o_ref.dtype)


def short_conv(ze, conv_w, ts_pref=512, halo=16):
    b, s, _ = ze.shape
    ts = _pick(s, ts_pref)
    cw = 4 * LANES
    nc = CONV_DIM // cw

    def cur(col):
        return pl.BlockSpec((1, ts, cw), lambda bi, i, c: (bi, i, col * LANES // cw + c))

    def prev(col):
        return pl.BlockSpec((1, halo, cw), lambda bi, i, c: (bi, jnp.maximum(i * (ts // halo) - 1, 0), col * LANES // cw + c))

    return pl.pallas_call(
        _conv_kernel,
        grid=(b, s // ts, nc),
        in_specs=[cur(ZE_GB), cur(ZE_GC), cur(ZE_HC), prev(ZE_GC), prev(ZE_HC),
                  pl.BlockSpec((CONV_K, cw), lambda bi, i, c: (0, c))],
        out_specs=pl.BlockSpec((1, ts, cw), lambda bi, i, c: (bi, i, c)),
        out_shape=jax.ShapeDtypeStruct((b, s, CONV_DIM), BF16),
        compiler_params=_cparams("parallel", "parallel", "parallel"),
        name="short_conv",
    )(ze, ze, ze, ze, ze, conv_w.astype(F32))


def _swa_kernel(q_ref, kp_ref, kc_ref, vp_ref, vc_ref, sink_ref, tb_ref, o_ref):
    i = pl.program_id(1)
    tq = QBLK
    half = LANES // 2
    nt = (((1,), (1,)), ((), ()))
    q = q_ref[0]
    k_all = jnp.concatenate([kp_ref[0], kc_ref[0]], axis=0).astype(F32) * (C_DH ** -0.5)
    v_all = jnp.concatenate([vp_ref[0], vc_ref[0]], axis=0).astype(F32)
    in_lo = lax.broadcasted_iota(jnp.int32, (2 * tq, LANES), 1) < half
    row_lo = lax.broadcasted_iota(jnp.int32, (LANES, tq), 0) < half
    ones = jnp.ones((LANES, 2 * tq), BF16)
    prev_tile = jnp.where(i > 0, 1, 2)
    pairs_per_kv = C_GRP // 2
    blocks = []
    for kv in range(C_KV):
        lanes = slice((kv // 2) * LANES, (kv // 2 + 1) * LANES)
        own = in_lo if kv % 2 == 0 else ~in_lo
        k_own = jnp.where(own, k_all[:, lanes], 0.0)
        v_own = jnp.where(own, v_all[:, lanes], 0.0)
        k_par = [k_own, pltpu.roll(k_own, half, 1)]
        v_par = [v_own, pltpu.roll(v_own, half, 1)]
        if kv % 2:
            k_par.reverse()
            v_par.reverse()
        q2 = jnp.concatenate([q[:, (kv * pairs_per_kv + m) * LANES:(kv * pairs_per_kv + m + 1) * LANES]
                              for m in range(pairs_per_kv)], axis=0)
        res_t = []
        for par in range(2):
            hs = [kv * C_GRP + 2 * m + par for m in range(pairs_per_kv)]
            s_t = lax.dot_general(k_par[par].astype(BF16), q2, nt, preferred_element_type=F32)
            bias_t = jnp.concatenate([jnp.concatenate([tb_ref[h, prev_tile], tb_ref[h, 0]], axis=0) for h in hs], axis=1)
            sink = jnp.concatenate([jnp.full((1, tq), sink_ref[h], F32) for h in hs], axis=1)
            s_t = s_t + bias_t
            m_col = jnp.maximum(jnp.max(s_t, axis=0, keepdims=True), sink)
            e_t = jnp.exp(s_t - m_col).astype(BF16)
            vo = jnp.concatenate([v_par[par].T.astype(BF16), ones], axis=0)
            nd = jnp.dot(vo, e_t, preferred_element_type=F32)
            res_t.append(nd[:LANES] / (nd[LANES:] + jnp.exp(sink - m_col)))
        both = jnp.where(jnp.concatenate([row_lo] * pairs_per_kv, axis=1), res_t[0], res_t[1])
        blocks.extend(both[:, m * tq:(m + 1) * tq].T for m in range(pairs_per_kv))
    o_ref[0] = jnp.concatenate(blocks, axis=-1).astype(o_ref.dtype)


def swa_bucket_idx():
    a = np.arange(QBLK)[:, None]
    c = np.arange(QBLK)[None, :]
    own = jnp.where(a - c >= 0, t5_bucket(a - c), -1).T
    before = jnp.where(QBLK + a - c < WIN_C, t5_bucket(QBLK + a - c), -1).T
    return jnp.stack([own, before, jnp.full((QBLK, QBLK), -1, jnp.int32)]).astype(jnp.int32)


def swa_attention(zo, sinks, tb):
    b, s, _ = zo.shape
    kcol = C_Q // C_KVW
    prev = lambda bi, i: jnp.maximum(i - 1, 0)
    return pl.pallas_call(
        _swa_kernel,
        grid=(b, s // QBLK),
        in_specs=[
            pl.BlockSpec((1, QBLK, C_Q), lambda bi, i: (bi, i, 0)),
            pl.BlockSpec((1, QBLK, C_KVW), lambda bi, i: (bi, prev(bi, i), kcol)),
            pl.BlockSpec((1, QBLK, C_KVW), lambda bi, i: (bi, i, kcol)),
            pl.BlockSpec((1, QBLK, C_KVW), lambda bi, i: (bi, prev(bi, i), kcol + 1)),
            pl.BlockSpec((1, QBLK, C_KVW), lambda bi, i: (bi, i, kcol + 1)),
            pl.BlockSpec(memory_space=pltpu.SMEM),
            pl.BlockSpec((C_HEADS, 3, QBLK, QBLK), lambda bi, i: (0, 0, 0, 0)),
        ],
        out_specs=pl.BlockSpec((1, QBLK, C_Q), lambda bi, i: (bi, i, 0)),
        out_shape=jax.ShapeDtypeStruct((b, s, C_Q), BF16),
        compiler_params=_cparams("parallel", "parallel"),
        name="swa_attention",
    )(zo, zo, zo, zo, zo, sinks.astype(F32), tb)


def _router_kernel(h_ref, g_ref, wr_ref, xn_ref, idx_ref, wt_ref, *, n_experts):
    x = h_ref[...]
    xn = x * lax.rsqrt(jnp.mean(x * x, axis=-1, keepdims=True) + EPS) * g_ref[...]
    xn_ref[...] = xn
    logits = jnp.dot(xn, wr_ref[...], preferred_element_type=F32, precision=lax.Precision.HIGHEST)
    lane = lax.broadcasted_iota(jnp.int32, logits.shape, 1)
    lg = jnp.where(lane < n_experts, logits, NEG)
    m1 = jnp.max(lg, axis=-1, keepdims=True)
    i1 = jnp.min(jnp.where(lg == m1, lane, LANES), axis=-1, keepdims=True)
    lg2 = jnp.where(lane == i1, NEG, lg)
    m2 = jnp.max(lg2, axis=-1, keepdims=True)
    i2 = jnp.min(jnp.where(lg2 == m2, lane, LANES), axis=-1, keepdims=True)
    e2 = jnp.exp(m2 - m1)
    idx_ref[...] = jnp.where(lane == 0, i1, jnp.where(lane == 1, i2, 0))
    wt_ref[...] = jnp.where(lane == 0, 1.0 / (1.0 + e2), jnp.where(lane == 1, e2 / (1.0 + e2), 0.0))


def router(h, g, w_router):
    t, d = h.shape
    e = w_router.shape[1]
    tm = _pick(t, 512)
    wr = jnp.zeros((d, LANES), F32).at[:, :e].set(w_router.astype(F32))
    row = lambda i: (i, 0)
    return pl.pallas_call(
        functools.partial(_router_kernel, n_experts=e),
        grid=(t // tm,),
        in_specs=[pl.BlockSpec((tm, d), row), pl.BlockSpec((1, d), lambda i: (0, 0)),
                  pl.BlockSpec((d, LANES), lambda i: (0, 0))],
        out_specs=[pl.BlockSpec((tm, d), row), pl.BlockSpec((tm, LANES), row), pl.BlockSpec((tm, LANES), row)],
        out_shape=[jax.ShapeDtypeStruct((t, d), F32), jax.ShapeDtypeStruct((t, LANES), jnp.int32),
                   jax.ShapeDtypeStruct((t, LANES), F32)],
        compiler_params=_cparams("parallel"),
        name="moe_router",
    )(h, g.reshape(1, d).astype(F32), wr)


def _row_copy(src_hbm, dst_ref, sem, src_row, dst_row):
    return pltpu.make_async_copy(src_hbm.at[pl.ds(src_row, 1), :], dst_ref.at[pl.ds(dst_row, 1), :], sem)


def _gather_kernel(tok_ref, x_hbm, o_ref, buf_ref, sem):
    rows = o_ref.shape[0]

    def start(r2, c):
        for par in range(2):
            r = 2 * r2 + par
            _row_copy(x_hbm, buf_ref, sem, tok_ref[0, 0, r], r).start(priority=par)
        return c

    def wait(r, c):
        _row_copy(x_hbm, buf_ref, sem, 0, r).wait()
        return c

    lax.fori_loop(0, rows // 2, start, 0)
    lax.fori_loop(0, rows, wait, 0)
    o_ref[...] = buf_ref[...].astype(o_ref.dtype)


def gather_rows(x, row_tok, tm, out_dtype):
    n_rows = row_tok.shape[0]
    d = x.shape[1]
    return pl.pallas_call(
        _gather_kernel,
        grid=(n_rows // tm,),
        in_specs=[pl.BlockSpec((1, 1, tm), lambda i: (i, 0, 0), memory_space=pltpu.SMEM),
                  pl.BlockSpec(memory_space=pl.ANY)],
        out_specs=pl.BlockSpec((tm, d), lambda i: (i, 0)),
        out_shape=jax.ShapeDtypeStruct((n_rows, d), out_dtype),
        scratch_shapes=[pltpu.VMEM((tm, d), x.dtype), pltpu.SemaphoreType.DMA(())],
        compiler_params=_cparams("arbitrary"),
        name="moe_gather",
    )(row_tok.reshape(n_rows // tm, 1, tm), x)


def _combine_kernel(p0_ref, p1_ref, y_hbm, wt_ref, h_ref, o_ref, a_ref, b_ref, sem):
    rows = o_ref.shape[0]

    def start(r, c):
        _row_copy(y_hbm, a_ref, sem.at[0], p0_ref[0, 0, r], r).start(priority=0)
        _row_copy(y_hbm, b_ref, sem.at[1], p1_ref[0, 0, r], r).start(priority=1)
        return c

    def wait(r, c):
        _row_copy(y_hbm, a_ref, sem.at[0], 0, r).wait()
        _row_copy(y_hbm, b_ref, sem.at[1], 0, r).wait()
        return c

    lax.fori_loop(0, rows, start, 0)
    lax.fori_loop(0, rows, wait, 0)
    wt = wt_ref[...]
    o_ref[...] = h_ref[...] + wt[:, 0:1] * a_ref[...] + wt[:, 1:2] * b_ref[...]


def combine_rows(y_rows, pos0, pos1, wt, h, tm):
    t, d = h.shape
    idx_spec = pl.BlockSpec((1, 1, tm), lambda i: (i, 0, 0), memory_space=pltpu.SMEM)
    return pl.pallas_call(
        _combine_kernel,
        grid=(t // tm,),
        in_specs=[idx_spec, idx_spec, pl.BlockSpec(memory_space=pl.ANY),
                  pl.BlockSpec((tm, LANES), lambda i: (i, 0)), pl.BlockSpec((tm, d), lambda i: (i, 0))],
        out_specs=pl.BlockSpec((tm, d), lambda i: (i, 0)),
        out_shape=jax.ShapeDtypeStruct((t, d), F32),
        scratch_shapes=[pltpu.VMEM((tm, d), F32), pltpu.VMEM((tm, d), F32), pltpu.SemaphoreType.DMA((2,))],
        compiler_params=_cparams("arbitrary"),
        name="moe_combine",
    )(pos0.reshape(t // tm, 1, tm), pos1.reshape(t // tm, 1, tm), y_rows, wt, h)


def moe_layer(h, g, w_router, wg, wu, wd, layer, tm_pref=512):
    t, d = h.shape
    n_exp = w_router.shape[1]
    tm = _pick(t, tm_pref)
    xn, idx, wt = router(h, g, w_router)
    e_flat = idx[:, :TOP_K].reshape(-1)
    onehot = (e_flat[:, None] == jnp.arange(n_exp)[None, :]).astype(jnp.int32)
    csum = jnp.cumsum(onehot, axis=0)
    rank = jnp.sum((csum - onehot) * onehot, axis=1)
    counts = csum[-1]
    padded = (counts + tm - 1) // tm * tm
    pad_end = jnp.cumsum(padded)
    dest = (pad_end - padded)[e_flat] + rank
    n_rows = t * TOP_K + n_exp * tm
    row_tok = jnp.zeros((n_rows,), jnp.int32).at[dest].set(jnp.arange(t * TOP_K, dtype=jnp.int32) // TOP_K)
    n_blk = n_rows // tm
    blk = jnp.arange(n_blk)
    n_used = (pad_end[-1:] // tm).astype(jnp.int32)
    blk_expert = jnp.minimum(jnp.searchsorted(pad_end, blk * tm, side='right'), n_exp - 1).astype(jnp.int32)
    blk_expert = jnp.where(blk < n_used[0], blk_expert, blk_expert[n_used[0] - 1])
    first = ((blk == 0) | (blk_expert != jnp.roll(blk_expert, 1))).astype(jnp.int32)
    groups = (layer * n_exp + blk_expert, first, n_used)

    xs = gather_rows(xn, row_tok, tm, BF16)
    hid = grouped_glu(xs, wg, wu, groups, tm)
    y_rows = grouped_matmul([hid], wd, groups, F32, tm, 512)
    pos = dest.reshape(t, TOP_K).astype(jnp.int32)
    return combine_rows(y_rows, pos[:, 0], pos[:, 1], wt, h, tm)


def _even_in_proj_weight(w_in):
    d = w_in.shape[0]
    g0 = A_Q + 6 * A_KVW
    gates = w_in[:, g0:g0 + 3 * A_HEADS].reshape(d, 3, A_KV, A_GRP)
    blocks = [w_in[:, :g0], w_in[:, g0 + 3 * A_HEADS:]]
    for k in range(A_KV):
        gk = gates[:, :, k, :].reshape(d, 3 * A_GRP)
        blocks.append(jnp.pad(gk, ((0, 0), (0, LANES - 3 * A_GRP))))
    w = jnp.concatenate(blocks, axis=1)
    return jnp.pad(w, ((0, 0), (0, ZE_WIDTH - w.shape[1])))


def kernel(x, p, rel_bias, norm_mix, norm_ffn, norm_ple, norm_final, w_in_e, cmp_pos, cmp_w1, cmp_w2, conv_w,
           w_out_e, w_gate_d, w_up_d, w_down_d, w_qkv_o, sinks, w_out_o, w_router, w_gate_m, w_up_m, w_down_m,
           w_ple, w_ple_gate):
    b, s, d = x.shape
    t = b * s
    depth = norm_mix.shape[0]
    n_cmp_pad = s // CMP_STRIDE
    tm = _pick(t, 1024)
    tm_down = _pick(t, 512)

    tw = bias_table(tile_bucket_idx(FAR_REL + 1), rel_bias, 0, A_HEADS)
    tw = tw.reshape(A_KV, A_GRP, FAR_REL + 1, QBLK, QBLK).transpose(0, 2, 1, 3, 4)
    tc = bias_table(cmp_bucket_idx(s, n_cmp_pad), rel_bias, 0, A_HEADS)
    tc = tc.reshape(A_KV, A_GRP, s // QBLK, QBLK, n_cmp_pad)
    tb = bias_table(swa_bucket_idx(), rel_bias, A_HEADS, C_HEADS)

    w_in_relaid = jnp.stack([_even_in_proj_weight(w_in_e[j]) for j in range(w_in_e.shape[0])])
    w_gate_m = w_gate_m.reshape((-1,) + w_gate_m.shape[2:])
    w_up_m = w_up_m.reshape((-1,) + w_up_m.shape[2:])
    w_down_m = w_down_m.reshape((-1,) + w_down_m.shape[2:])
    p = p.reshape(depth, t, -1)

    h = x.reshape(t, d).astype(F32)
    for i in range(depth):
        j = i // 2
        layer = _dense_groups(t // tm, j)
        hn = rmsnorm(h, norm_mix[i], BF16)
        if i % 2 == 0:
            ze = grouped_matmul([hn], w_in_relaid, layer, BF16, tm, 1024).reshape(b, s, ZE_WIDTH)
            kvcm = compress(ze, cmp_pos[j], cmp_w1[j], cmp_w2[j])
            att = nsa_attention(ze, kvcm, tw, tc).reshape(t, A_Q)
            cnv = short_conv(ze, conv_w[j]).reshape(t, CONV_DIM)
            h = grouped_matmul([att, cnv], w_out_e, layer, F32, tm, 512, res=h)
            hn = rmsnorm(h, norm_ffn[i], BF16)
            hid = grouped_glu(hn, w_gate_d, w_up_d, layer, tm)
            h = grouped_matmul([hid], w_down_d, _dense_groups(t // tm_down, j), F32, tm_down, 512, res=h)
        else:
            zo = grouped_matmul([hn], w_qkv_o, layer, BF16, tm, 1280).reshape(b, s, -1)
            att = swa_attention(zo, sinks[j], tb).reshape(t, C_Q)
            h = grouped_matmul([att], w_out_o, layer, F32, tm, 512, res=h)
            h = moe_layer(h, norm_ffn[i], w_router[j], w_gate_m, w_up_m, w_down_m, j)
        hn = rmsnorm(h, norm_ple[i], BF16)
        h = ple(hn, p, w_ple_gate, w_ple, i, h)
    return rmsnorm(h, norm_final, x.dtype).reshape(b, s, d)
```

```python
import functools
import math

import numpy as np
import jax
import jax.numpy as jnp
from jax import lax
from jax.experimental import pallas as pl
from jax.experimental.pallas import tpu as pltpu

F32 = jnp.float32
BF16 = jnp.bfloat16

LANES = 128
VMEM_LIMIT_BYTES = 56 * 1024 * 1024

EPS = 1e-6
A_HEADS = 8
A_KV = 2
A_GRP = A_HEADS // A_KV
A_DH = 128
A_Q = A_HEADS * A_DH
A_KVW = A_KV * A_DH
CMP_LEN = 32
CMP_STRIDE = 16
SLC_BLK = 64
SLC_TOPK = 16
WIN_A = 512
CONV_DIM = 1024
CONV_K = 3
C_HEADS = 32
C_KV = 4
C_GRP = C_HEADS // C_KV
C_DH = 64
C_Q = C_HEADS * C_DH
C_KVW = C_KV * C_DH
WIN_C = 128
N_BUCKETS = 32
MAX_DIST = 1024
TOP_K = 2

QBLK = 128
FAR_REL = 8
NEG = -1e30
BIG = 1e30

ZE_Q = 0
ZE_KC, ZE_VC, ZE_KS, ZE_VS, ZE_KW, ZE_VW = 8, 10, 12, 14, 16, 18
ZE_GB, ZE_GC, ZE_HC = 20, 28, 36
ZE_GATE = 44
ZE_WIDTH = 48 * LANES


def _cparams(*sem):
    return pltpu.CompilerParams(dimension_semantics=sem, vmem_limit_bytes=VMEM_LIMIT_BYTES)


def _sigmoid(v):
    return 1.0 / (1.0 + jnp.exp(-v))


def _pick(n, pref):
    t = min(n, pref)
    while n % t:
        t -= LANES if t > LANES else 8
    return t


def _pack_bf16_pairs(x):
    half = x.shape[1] // 2
    lo = lax.bitcast_convert_type(x[:, :half].astype(BF16).astype(F32), jnp.uint32)
    hi = lax.bitcast_convert_type(x[:, half:].astype(BF16).astype(F32), jnp.uint32)
    return lax.shift_right_logical(lo, jnp.uint32(16)) | hi


def _unpack_bf16_pairs(w):
    lo = lax.bitcast_convert_type(lax.shift_left(w, jnp.uint32(16)), F32)
    hi = lax.bitcast_convert_type(w & jnp.uint32(0xFFFF0000), F32)
    return jnp.concatenate([lo, hi], axis=1)


def _rmsnorm_kernel(x_ref, g_ref, o_ref):
    x = x_ref[...]
    y = x * lax.rsqrt(jnp.mean(x * x, axis=-1, keepdims=True) + EPS)
    o_ref[...] = (y * g_ref[...]).astype(o_ref.dtype)


def rmsnorm(x, g, out_dtype):
    t, d = x.shape
    tm = _pick(t, 512)
    return pl.pallas_call(
        _rmsnorm_kernel,
        grid=(t // tm,),
        in_specs=[pl.BlockSpec((tm, d), lambda i: (i, 0)), pl.BlockSpec((1, d), lambda i: (0, 0))],
        out_specs=pl.BlockSpec((tm, d), lambda i: (i, 0)),
        out_shape=jax.ShapeDtypeStruct((t, d), out_dtype),
        compiler_params=_cparams("parallel"),
        name="rmsnorm",
    )(x, g.reshape(1, d).astype(F32))


def _dense_groups(n_blk, layer):
    grp = jnp.full((n_blk,), layer, jnp.int32)
    first = (jnp.arange(n_blk) == 0).astype(jnp.int32)
    return grp, first, jnp.full((1,), n_blk, jnp.int32)


def _gmm_kernel(grp_ref, first_ref, nu_ref, *refs, k_sizes, has_res, pack_out):
    n_x = len(k_sizes)
    w_ref = refs[n_x]
    o_ref, wb_ref = refs[-2], refs[-1]
    i = pl.program_id(1)

    @pl.when(first_ref[i] == 1)
    def _():
        wb_ref[...] = w_ref[...].astype(BF16)

    @pl.when(i < nu_ref[0])
    def _():
        acc = None
        off = 0
        for x_ref, k in zip(refs[:n_x], k_sizes):
            part = jnp.dot(x_ref[...], wb_ref[off:off + k, :], preferred_element_type=F32)
            acc = part if acc is None else acc + part
            off += k
        if has_res:
            acc = refs[n_x + 1][...] + acc
        o_ref[...] = _pack_bf16_pairs(acc) if pack_out else acc.astype(o_ref.dtype)

    @pl.when(i >= nu_ref[0])
    def _():
        o_ref[...] = jnp.zeros(o_ref.shape, o_ref.dtype)


def grouped_matmul(xs, w, groups, out_dtype, tm, tn_pref, res=None, pack_out=False):
    t = xs[0].shape[0]
    kw, n = w.shape[1], w.shape[2]
    k_sizes = tuple(x.shape[1] for x in xs)
    tn = _pick(n, tn_pref)
    in_specs = [pl.BlockSpec((tm, k), lambda j, i, g, f, nu: (i, 0)) for k in k_sizes]
    in_specs.append(pl.BlockSpec((None, kw, tn), lambda j, i, g, f, nu: (g[i], 0, j)))
    args = list(xs) + [w]
    if res is not None:
        in_specs.append(pl.BlockSpec((tm, tn), lambda j, i, g, f, nu: (i, j)))
        args.append(res)
    grid_spec = pltpu.PrefetchScalarGridSpec(
        num_scalar_prefetch=3,
        grid=(n // tn, t // tm),
        in_specs=in_specs,
        out_specs=pl.BlockSpec((tm, tn // 2 if pack_out else tn), lambda j, i, g, f, nu: (i, j)),
        scratch_shapes=[pltpu.VMEM((kw, tn), BF16)],
    )
    out_shape = jax.ShapeDtypeStruct((t, n // 2), jnp.uint32) if pack_out else jax.ShapeDtypeStruct((t, n), out_dtype)
    return pl.pallas_call(
        functools.partial(_gmm_kernel, k_sizes=k_sizes, has_res=res is not None, pack_out=pack_out),
        grid_spec=grid_spec,
        out_shape=out_shape,
        compiler_params=_cparams("arbitrary", "arbitrary"),
        name="matmul",
    )(*groups, *args)


def _glu_kernel(grp_ref, first_ref, nu_ref, x_ref, wg_ref, wu_ref, o_ref, wgb_ref, wub_ref):
    i = pl.program_id(1)

    @pl.when(first_ref[i] == 1)
    def _():
        wgb_ref[...] = wg_ref[...].astype(BF16)
        wub_ref[...] = wu_ref[...].astype(BF16)

    @pl.when(i < nu_ref[0])
    def _():
        x = x_ref[...]
        g = jnp.dot(x, wgb_ref[...], preferred_element_type=F32)
        u = jnp.dot(x, wub_ref[...], preferred_element_type=F32)
        o_ref[...] = (g * _sigmoid(g) * u).astype(o_ref.dtype)

    @pl.when(i >= nu_ref[0])
    def _():
        o_ref[...] = jnp.zeros(o_ref.shape, o_ref.dtype)


def grouped_glu(x, wg, wu, groups, tm, tf_pref=512):
    t, d = x.shape
    f = wg.shape[2]
    tf = _pick(f, tf_pref)
    w_spec = pl.BlockSpec((None, d, tf), lambda j, i, g, fl, nu: (g[i], 0, j))
    grid_spec = pltpu.PrefetchScalarGridSpec(
        num_scalar_prefetch=3,
        grid=(f // tf, t // tm),
        in_specs=[pl.BlockSpec((tm, d), lambda j, i, g, fl, nu: (i, 0)), w_spec, w_spec],
        out_specs=pl.BlockSpec((tm, tf), lambda j, i, g, fl, nu: (i, j)),
        scratch_shapes=[pltpu.VMEM((d, tf), BF16), pltpu.VMEM((d, tf), BF16)],
    )
    return pl.pallas_call(
        _glu_kernel,
        grid_spec=grid_spec,
        out_shape=jax.ShapeDtypeStruct((t, f), BF16),
        compiler_params=_cparams("arbitrary", "arbitrary"),
        name="glu",
    )(*groups, x, wg, wu)


def _ple_kernel(hn_ref, p_ref, wg_ref, wp_ref, h_ref, o_ref, wgb_ref, wpb_ref):
    @pl.when(pl.program_id(1) == 0)
    def _():
        wgb_ref[...] = wg_ref[...].astype(BF16)
        wpb_ref[...] = wp_ref[...].astype(BF16)

    gate = _sigmoid(jnp.dot(hn_ref[...], wgb_ref[...], preferred_element_type=F32))
    pe = jnp.dot(p_ref[...].astype(BF16), wpb_ref[...], preferred_element_type=F32)
    o_ref[...] = h_ref[...] + gate * pe


def ple(hn, p, wg, wp, layer, h, tm_pref=512, tn_pref=1024):
    t, d = h.shape
    pd = p.shape[2]
    tm, tn = _pick(t, tm_pref), _pick(d, tn_pref)
    return pl.pallas_call(
        _ple_kernel,
        grid=(d // tn, t // tm),
        in_specs=[
            pl.BlockSpec((tm, d), lambda j, i: (i, 0)),
            pl.BlockSpec((None, tm, pd), lambda j, i: (layer, i, 0)),
            pl.BlockSpec((None, d, tn), lambda j, i: (layer, 0, j)),
            pl.BlockSpec((None, pd, tn), lambda j, i: (layer, 0, j)),
            pl.BlockSpec((tm, tn), lambda j, i: (i, j)),
        ],
        out_specs=pl.BlockSpec((tm, tn), lambda j, i: (i, j)),
        out_shape=jax.ShapeDtypeStruct((t, d), F32),
        scratch_shapes=[pltpu.VMEM((d, tn), BF16), pltpu.VMEM((pd, tn), BF16)],
        compiler_params=_cparams("arbitrary", "arbitrary"),
        name="ple",
    )(hn, p, wg, wp, h)


def t5_bucket(dist):
    n = jnp.maximum(jnp.asarray(dist, jnp.int32), 0)
    exact = N_BUCKETS // 2
    nf = jnp.maximum(n, 1).astype(F32)
    large = exact + (jnp.log(nf / exact) / math.log(MAX_DIST / exact) * (N_BUCKETS - exact)).astype(jnp.int32)
    return jnp.where(n < exact, n, jnp.minimum(large, N_BUCKETS - 1))


def _bias_table_kernel(idx_ref, tab_ref, o_ref, *, head0):
    h = head0 + pl.program_id(0)
    idx = idx_ref[0]
    val = jnp.full(idx.shape, tab_ref[0, h], F32)
    for b in range(1, N_BUCKETS):
        val = jnp.where(idx == b, tab_ref[b, h], val)
    o_ref[0, 0] = jnp.where(idx < 0, NEG, val)


def bias_table(bucket_idx, rel_bias, head0, n_heads):
    n, r, c = bucket_idx.shape
    return pl.pallas_call(
        functools.partial(_bias_table_kernel, head0=head0),
        grid=(n_heads, n),
        in_specs=[
            pl.BlockSpec((1, r, c), lambda h, i: (i, 0, 0)),
            pl.BlockSpec(memory_space=pltpu.SMEM),
        ],
        out_specs=pl.BlockSpec((1, 1, r, c), lambda h, i: (h, i, 0, 0)),
        out_shape=jax.ShapeDtypeStruct((n_heads, n, r, c), F32),
        compiler_params=_cparams("parallel", "parallel"),
        name="bias_table",
    )(bucket_idx, rel_bias.astype(F32))


def tile_bucket_idx(n_rel):
    a = np.arange(QBLK)[None, :, None]
    c = np.arange(QBLK)[None, None, :]
    r = np.arange(n_rel)[:, None, None]
    return t5_bucket(QBLK * r + a - c)


def cmp_bucket_idx(s, n_cmp_pad):
    t = np.arange(s).reshape(s // QBLK, QBLK, 1)
    cend = (np.arange(n_cmp_pad) * CMP_STRIDE + CMP_LEN - 1)[None, None, :]
    return t5_bucket(t - cend)


def _compress_kernel(t_ref, pos_ref, w1_ref, w2_ref, o_ref, tf_ref, *, n_pad):
    s = t_ref.shape[1]
    half = CMP_LEN // 2
    tf_ref[0:s, :] = t_ref[0].astype(F32)
    tf_ref[s:s + half, :] = jnp.zeros((half, A_DH), F32)
    hid = jnp.zeros((n_pad, w1_ref.shape[2]), F32)
    for l in range(CMP_LEN):
        rows = tf_ref[pl.ds(l, n_pad, stride=CMP_STRIDE), :] + pos_ref[0, l:l + 1, :]
        hid = hid + jnp.dot(rows.astype(BF16), w1_ref[0, l * A_DH:(l + 1) * A_DH, :], preferred_element_type=F32)
    act = jax.nn.gelu(hid)
    out = jnp.dot(act.astype(BF16), w2_ref[0], preferred_element_type=F32)
    valid = lax.broadcasted_iota(jnp.int32, out.shape, 0) < n_pad - 1
    o_ref[0, 0, 0] = jnp.where(valid, out, 0.0).astype(o_ref.dtype)


def compress(ze, cmp_pos, cmp_w1, cmp_w2):
    b, s, _ = ze.shape
    n_pad = s // CMP_STRIDE
    hid = cmp_w1.shape[2]
    return pl.pallas_call(
        functools.partial(_compress_kernel, n_pad=n_pad),
        grid=(2, b, A_KV),
        in_specs=[
            pl.BlockSpec((1, s, A_DH), lambda w, bi, k: (bi, 0, ZE_KC + 2 * w + k)),
            pl.BlockSpec((1, CMP_LEN, A_DH), lambda w, bi, k: (w, 0, 0)),
            pl.BlockSpec((1, CMP_LEN * A_DH, hid), lambda w, bi, k: (w, 0, 0)),
            pl.BlockSpec((1, hid, A_DH), lambda w, bi, k: (w, 0, 0)),
        ],
        out_specs=pl.BlockSpec((1, 1, 1, n_pad, A_DH), lambda w, bi, k: (w, bi, k, 0, 0)),
        out_shape=jax.ShapeDtypeStruct((2, b, A_KV, n_pad, A_DH), BF16),
        scratch_shapes=[pltpu.VMEM((s + CMP_LEN // 2, A_DH), F32)],
        compiler_params=_cparams("parallel", "parallel", "parallel"),
        name="nsa_compress",
    )(ze, cmp_pos.astype(F32), cmp_w1.astype(BF16), cmp_w2.astype(BF16))


def _nsa_kernel(q_ref, kcm_ref, vcm_ref, ks_ref, vs_ref, kw_ref, vw_ref, g_ref, tw_ref, tc_ref, o_ref,
                qs_ref, vct_ref, vst_ref, vwt_ref, sel_ref, m_ref, l_ref, acc_ref):
    i = pl.program_id(2)
    tq = QBLK
    hq = A_GRP * tq
    s_len = ks_ref.shape[1]
    n_cmp = kcm_ref.shape[3]
    n_slc = s_len // SLC_BLK
    nt = (((1,), (1,)), ((), ()))
    heads = [slice(g * tq, (g + 1) * tq) for g in range(A_GRP)]

    def per_head(x):
        return jnp.concatenate([x] * A_GRP, axis=1)

    @pl.when(i == 0)
    def _():
        vct_ref[...] = vcm_ref[0, 0, 0].astype(F32).T.astype(BF16)

        def transpose_tile(t, c):
            off = pl.multiple_of(t * tq, tq)
            vst_ref[:, pl.ds(off, tq)] = vs_ref[0, pl.ds(off, tq), :].astype(F32).T.astype(BF16)
            vwt_ref[:, pl.ds(off, tq)] = vw_ref[0, pl.ds(off, tq), :].astype(F32).T.astype(BF16)
            return c

        lax.fori_loop(0, s_len // tq, transpose_tile, 0)

    q = q_ref[0]
    for g in range(A_GRP):
        qs_ref[heads[g], :] = (q[:, g * A_DH:(g + 1) * A_DH].astype(F32) * (A_DH ** -0.5)).astype(BF16)
    t_row = i * tq + lax.broadcasted_iota(jnp.int32, (1, tq), 1)

    n_col = lax.broadcasted_iota(jnp.int32, (n_cmp, hq), 0)
    t_all = i * tq + (lax.broadcasted_iota(jnp.int32, (n_cmp, hq), 1) & (tq - 1))
    mask_c = (t_all >= n_col * CMP_STRIDE + (CMP_LEN - 1)) & (n_col < n_cmp - 1)
    s_c = lax.dot_general(kcm_ref[0, 0, 0], qs_ref[...], nt, preferred_element_type=F32) + tc_ref[0, 0]
    s_c = jnp.where(mask_c, s_c, NEG)
    e_c = jnp.where(mask_c, jnp.exp(s_c - jnp.max(s_c, axis=0, keepdims=True)), 0.0)
    r_c = 1.0 / jnp.maximum(jnp.sum(e_c, axis=0, keepdims=True), 1e-30)
    o_c = jnp.dot(vct_ref[...], e_c.astype(BF16), preferred_element_type=F32) * r_c
    p_c = e_c * r_c
    p_sum = (p_c[:, heads[0]] + p_c[:, heads[1]]) + (p_c[:, heads[2]] + p_c[:, heads[3]])
    ov_j = lax.broadcasted_iota(jnp.int32, (LANES, n_cmp), 0) * SLC_BLK
    ov_n = lax.broadcasted_iota(jnp.int32, (LANES, n_cmp), 1) * CMP_STRIDE
    overlap_t = ((ov_n < ov_j + SLC_BLK) & (ov_n + CMP_LEN > ov_j)).astype(F32)
    imp_t = jnp.dot(overlap_t, p_sum, preferred_element_type=F32, precision=lax.Precision.HIGHEST)

    jb = lax.broadcasted_iota(jnp.int32, (LANES, tq), 0)
    cur = t_row // SLC_BLK
    imp_t = jnp.where(jb * SLC_BLK > t_row, -BIG, imp_t)
    imp_t = jnp.where((jb == 0) | (jb == cur) | (jb == cur - 1), BIG, imp_t)
    sub = 8
    slabs = [imp_t[r * sub:(r + 1) * sub] for r in range(n_slc // sub)]
    jb_slab = lax.broadcasted_iota(jnp.int32, (sub, tq), 0)
    ranks = [jnp.zeros((sub, tq), F32) for _ in slabs]
    for j2 in range(n_slc):
        other = imp_t[j2:j2 + 1, :]
        for r, v in enumerate(slabs):
            if r * sub > j2:
                ahead = other >= v
            elif r * sub + sub - 1 <= j2:
                ahead = other > v
            else:
                ahead = (other > v) | ((other == v) & (jb_slab + r * sub > j2))
            ranks[r] = ranks[r] + jnp.where(ahead, 1.0, 0.0)
    sel_ref[...] = jnp.concatenate([jnp.where(rk < float(min(SLC_TOPK, n_slc)), 1.0, 0.0) for rk in ranks], axis=0)

    n_w = WIN_A // tq + 1
    jw0 = jnp.maximum(i - WIN_A // tq, 0)
    w_start = pl.multiple_of(jw0 * tq, tq)
    dist_w = t_row - (w_start + lax.broadcasted_iota(jnp.int32, (n_w * tq, tq), 0))
    madd_w = jnp.where((dist_w >= 0) & (dist_w < WIN_A), 0.0, NEG)
    s_w = lax.dot_general(kw_ref[0, pl.ds(w_start, n_w * tq), :], qs_ref[...], nt, preferred_element_type=F32)
    s_w = s_w + jnp.concatenate([tw_ref[0, jnp.clip(i - jw0 - cb, 0, FAR_REL)] for cb in range(n_w)], axis=0)
    s_w = s_w + per_head(madd_w)
    e_w = jnp.exp(s_w - jnp.max(s_w, axis=0, keepdims=True))
    r_w = 1.0 / jnp.maximum(jnp.sum(e_w, axis=0, keepdims=True), 1e-30)
    o_w = jnp.dot(vwt_ref[:, pl.ds(w_start, n_w * tq)], e_w.astype(BF16), preferred_element_type=F32) * r_w

    ck = 4 * tq
    blocks_per_chunk = ck // SLC_BLK
    m_ref[...] = jnp.full(m_ref.shape, NEG, F32)
    l_ref[...] = jnp.zeros(l_ref.shape, F32)
    acc_ref[...] = jnp.zeros(acc_ref.shape, F32)
    key_iota = lax.broadcasted_iota(jnp.int32, (ck, tq), 0)

    def chunk(c, carry):
        start = pl.multiple_of(c * ck, ck)
        sel_rows = sel_ref[pl.ds(pl.multiple_of(c * blocks_per_chunk, blocks_per_chunk), blocks_per_chunk), :]
        chosen = jnp.concatenate([jnp.broadcast_to(sel_rows[b:b + 1, :], (SLC_BLK, tq))
                                  for b in range(blocks_per_chunk)], axis=0)
        madd = jnp.where(jnp.where(start + key_iota <= t_row, chosen, 0.0) > 0.5, 0.0, NEG)
        s_s = lax.dot_general(ks_ref[0, pl.ds(start, ck), :], qs_ref[...], nt, preferred_element_type=F32)
        s_s = s_s + jnp.concatenate([tw_ref[0, jnp.clip(i - (ck // tq) * c - cb, 0, FAR_REL)]
                                     for cb in range(ck // tq)], axis=0)
        s_s = s_s + per_head(madd)
        m_old = m_ref[...]
        m_new = jnp.maximum(m_old, jnp.max(s_s, axis=0, keepdims=True))
        p = jnp.exp(s_s - m_new)
        alpha = jnp.exp(m_old - m_new)
        l_ref[...] = alpha * l_ref[...] + jnp.sum(p, axis=0, keepdims=True)
        acc_ref[...] = alpha * acc_ref[...] + jnp.dot(vst_ref[:, pl.ds(start, ck)], p.astype(BF16),
                                                      preferred_element_type=F32)
        m_ref[...] = m_new
        return carry

    lax.fori_loop(0, (i * tq) // ck + 1, chunk, 0)
    o_s = acc_ref[...] / jnp.maximum(l_ref[...], 1e-30)

    gate_t = _sigmoid(g_ref[0].astype(F32)).T
    outs = []
    for g in range(A_GRP):
        o_t = (gate_t[g:g + 1, :] * o_c[:, heads[g]]
               + gate_t[A_GRP + g:A_GRP + g + 1, :] * o_s[:, heads[g]]
               + gate_t[2 * A_GRP + g:2 * A_GRP + g + 1, :] * o_w[:, heads[g]])
        outs.append(o_t.T)
    o_ref[0] = jnp.concatenate(outs, axis=-1).astype(o_ref.dtype)


def nsa_attention(ze, kvcm, tw, tc):
    b, s, _ = ze.shape
    n_cmp = kvcm.shape[3]
    hq = A_GRP * QBLK

    def kv_spec(col):
        return pl.BlockSpec((1, s, A_DH), lambda bi, k, i: (bi, 0, col + k))

    return pl.pallas_call(
        _nsa_kernel,
        grid=(b, A_KV, s // QBLK),
        in_specs=[
            pl.BlockSpec((1, QBLK, A_GRP * A_DH), lambda bi, k, i: (bi, i, k)),
            pl.BlockSpec((1, 1, 1, n_cmp, A_DH), lambda bi, k, i: (0, bi, k, 0, 0)),
            pl.BlockSpec((1, 1, 1, n_cmp, A_DH), lambda bi, k, i: (1, bi, k, 0, 0)),
            kv_spec(ZE_KS), kv_spec(ZE_VS), kv_spec(ZE_KW), kv_spec(ZE_VW),
            pl.BlockSpec((1, QBLK, LANES), lambda bi, k, i: (bi, i, ZE_GATE + k)),
            pl.BlockSpec((1, FAR_REL + 1, QBLK, hq), lambda bi, k, i: (k, 0, 0, 0)),
            pl.BlockSpec((1, 1, n_cmp, hq), lambda bi, k, i: (k, i, 0, 0)),
        ],
        out_specs=pl.BlockSpec((1, QBLK, A_GRP * A_DH), lambda bi, k, i: (bi, i, k)),
        out_shape=jax.ShapeDtypeStruct((b, s, A_Q), BF16),
        scratch_shapes=[pltpu.VMEM((hq, A_DH), BF16), pltpu.VMEM((A_DH, n_cmp), BF16),
                        pltpu.VMEM((A_DH, s), BF16), pltpu.VMEM((A_DH, s), BF16),
                        pltpu.VMEM((s // SLC_BLK, QBLK), F32),
                        pltpu.VMEM((1, hq), F32), pltpu.VMEM((1, hq), F32), pltpu.VMEM((A_DH, hq), F32)],
        compiler_params=_cparams("parallel", "parallel", "arbitrary"),
        name="nsa_attention",
    )(ze, kvcm, kvcm, ze, ze, ze, ze, ze, tw, tc)


def nsa_bias_tiles(rel_bias, s):
    n_cmp = s // CMP_STRIDE
    tw = bias_table(jnp.swapaxes(tile_bucket_idx(FAR_REL + 1), 1, 2), rel_bias, 0, A_HEADS)
    tw = tw.reshape(A_KV, A_GRP, FAR_REL + 1, QBLK, QBLK).transpose(0, 2, 3, 1, 4)
    tc = bias_table(jnp.swapaxes(cmp_bucket_idx(s, n_cmp), 1, 2), rel_bias, 0, A_HEADS)
    tc = tc.reshape(A_KV, A_GRP, s // QBLK, n_cmp, QBLK).transpose(0, 2, 3, 1, 4)
    return (tw.reshape(A_KV, FAR_REL + 1, QBLK, A_GRP * QBLK), tc.reshape(A_KV, s // QBLK, n_cmp, A_GRP * QBLK))


def _conv_kernel(gb_ref, gc_ref, hc_ref, gcp_ref, hcp_ref, w_ref, o_ref):
    i = pl.program_id(1)
    u = gc_ref[0].astype(F32) * hc_ref[0].astype(F32)
    prev = gcp_ref[0].astype(F32) * hcp_ref[0].astype(F32)
    prev = jnp.where(i > 0, prev, 0.0)
    n_prev = prev.shape[0]
    p1 = prev[n_prev - 1:n_prev, :]
    p2 = prev[n_prev - 2:n_prev - 1, :]
    row = lax.broadcasted_iota(jnp.int32, u.shape, 0)
    u1 = jnp.where(row == 0, p1, pltpu.roll(u, 1, 0))
    u2 = jnp.where(row == 0, p2, jnp.where(row == 1, p1, pltpu.roll(u, 2, 0)))
    w = w_ref[...]
    y = w[0:1, :] * u2 + w[1:2, :] * u1 + w[2:3, :] * u
    o_ref[0] = (gb_ref[0].astype(F32) * y).astype(o_ref.dtype)


def short_conv(ze, conv_w, ts_pref=512, halo=16):
    b, s, _ = ze.shape
    ts = _pick(s, ts_pref)
    cw = 4 * LANES
    nc = CONV_DIM // cw

    def cur(col):
        return pl.BlockSpec((1, ts, cw), lambda bi, i, c: (bi, i, col * LANES // cw + c))

    def prev(col):
        return pl.BlockSpec((1, halo, cw), lambda bi, i, c: (bi, jnp.maximum(i * (ts // halo) - 1, 0), col * LANES // cw + c))

    return pl.pallas_call(
        _conv_kernel,
        grid=(b, s // ts, nc),
        in_specs=[cur(ZE_GB), cur(ZE_GC), cur(ZE_HC), prev(ZE_GC), prev(ZE_HC),
                  pl.BlockSpec((CONV_K, cw), lambda bi, i, c: (0, c))],
        out_specs=pl.BlockSpec((1, ts, cw), lambda bi, i, c: (bi, i, c)),
        out_shape=jax.ShapeDtypeStruct((b, s, CONV_DIM), BF16),
        compiler_params=_cparams("parallel", "parallel", "parallel"),
        name="short_conv",
    )(ze, ze, ze, ze, ze, conv_w.astype(F32))


def _swa_kernel(q_ref, kp_ref, kc_ref, vp_ref, vc_ref, sink_ref, tb_ref, o_ref):
    i = pl.program_id(1)
    tq = QBLK
    half = LANES // 2
    nt = (((1,), (1,)), ((), ()))
    q = q_ref[0]
    k_all = jnp.concatenate([kp_ref[0], kc_ref[0]], axis=0).astype(F32) * (C_DH ** -0.5)
    v_all = jnp.concatenate([vp_ref[0], vc_ref[0]], axis=0).astype(F32)
    in_lo = lax.broadcasted_iota(jnp.int32, (2 * tq, LANES), 1) < half
    row_lo = lax.broadcasted_iota(jnp.int32, (LANES, tq), 0) < half
    ones = jnp.ones((LANES, 2 * tq), BF16)
    prev_tile = jnp.where(i > 0, 1, 2)
    pairs_per_kv = C_GRP // 2
    blocks = []
    for kv in range(C_KV):
        lanes = slice((kv // 2) * LANES, (kv // 2 + 1) * LANES)
        own = in_lo if kv % 2 == 0 else ~in_lo
        k_own = jnp.where(own, k_all[:, lanes], 0.0)
        v_own = jnp.where(own, v_all[:, lanes], 0.0)
        k_par = [k_own, pltpu.roll(k_own, half, 1)]
        v_par = [v_own, pltpu.roll(v_own, half, 1)]
        if kv % 2:
            k_par.reverse()
            v_par.reverse()
        q2 = jnp.concatenate([q[:, (kv * pairs_per_kv + m) * LANES:(kv * pairs_per_kv + m + 1) * LANES]
                              for m in range(pairs_per_kv)], axis=0)
        res_t = []
        for par in range(2):
            hs = [kv * C_GRP + 2 * m + par for m in range(pairs_per_kv)]
            s_t = lax.dot_general(k_par[par].astype(BF16), q2, nt, preferred_element_type=F32)
            bias_t = jnp.concatenate([jnp.concatenate([tb_ref[h, prev_tile], tb_ref[h, 0]], axis=0) for h in hs], axis=1)
            sink = jnp.concatenate([jnp.full((1, tq), sink_ref[h], F32) for h in hs], axis=1)
            s_t = s_t + bias_t
            m_col = jnp.maximum(jnp.max(s_t, axis=0, keepdims=True), sink)
            e_t = jnp.exp(s_t - m_col).astype(BF16)
            vo = jnp.concatenate([v_par[par].T.astype(BF16), ones], axis=0)
            nd = jnp.dot(vo, e_t, preferred_element_type=F32)
            res_t.append(nd[:LANES] / (nd[LANES:] + jnp.exp(sink - m_col)))
        both = jnp.where(jnp.concatenate([row_lo] * pairs_per_kv, axis=1), res_t[0], res_t[1])
        blocks.extend(both[:, m * tq:(m + 1) * tq].T for m in range(pairs_per_kv))
    o_ref[0] = jnp.concatenate(blocks, axis=-1).astype(o_ref.dtype)


def swa_bucket_idx():
    a = np.arange(QBLK)[:, None]
    c = np.arange(QBLK)[None, :]
    own = jnp.where(a - c >= 0, t5_bucket(a - c), -1).T
    before = jnp.where(QBLK + a - c < WIN_C, t5_bucket(QBLK + a - c), -1).T
    return jnp.stack([own, before, jnp.full((QBLK, QBLK), -1, jnp.int32)]).astype(jnp.int32)


def swa_attention(zo, sinks, tb):
    b, s, _ = zo.shape
    kcol = C_Q // C_KVW
    prev = lambda bi, i: jnp.maximum(i - 1, 0)
    return pl.pallas_call(
        _swa_kernel,
        grid=(b, s // QBLK),
        in_specs=[
            pl.BlockSpec((1, QBLK, C_Q), lambda bi, i: (bi, i, 0)),
            pl.BlockSpec((1, QBLK, C_KVW), lambda bi, i: (bi, prev(bi, i), kcol)),
            pl.BlockSpec((1, QBLK, C_KVW), lambda bi, i: (bi, i, kcol)),
            pl.BlockSpec((1, QBLK, C_KVW), lambda bi, i: (bi, prev(bi, i), kcol + 1)),
            pl.BlockSpec((1, QBLK, C_KVW), lambda bi, i: (bi, i, kcol + 1)),
            pl.BlockSpec(memory_space=pltpu.SMEM),
            pl.BlockSpec((C_HEADS, 3, QBLK, QBLK), lambda bi, i: (0, 0, 0, 0)),
        ],
        out_specs=pl.BlockSpec((1, QBLK, C_Q), lambda bi, i: (bi, i, 0)),
        out_shape=jax.ShapeDtypeStruct((b, s, C_Q), BF16),
        compiler_params=_cparams("parallel", "parallel"),
        name="swa_attention",
    )(zo, zo, zo, zo, zo, sinks.astype(F32), tb)


def _router_kernel(h_ref, g_ref, wr_ref, xn_ref, idx_ref, wt_ref, *, n_experts):
    x = h_ref[...]
    xn = x * lax.rsqrt(jnp.mean(x * x, axis=-1, keepdims=True) + EPS) * g_ref[...]
    xn_ref[...] = _pack_bf16_pairs(xn)
    logits = jnp.dot(xn, wr_ref[...], preferred_element_type=F32, precision=lax.Precision.HIGHEST)
    lane = lax.broadcasted_iota(jnp.int32, logits.shape, 1)
    lg = jnp.where(lane < n_experts, logits, NEG)
    m1 = jnp.max(lg, axis=-1, keepdims=True)
    i1 = jnp.min(jnp.where(lg == m1, lane, LANES), axis=-1, keepdims=True)
    lg2 = jnp.where(lane == i1, NEG, lg)
    m2 = jnp.max(lg2, axis=-1, keepdims=True)
    i2 = jnp.min(jnp.where(lg2 == m2, lane, LANES), axis=-1, keepdims=True)
    e2 = jnp.exp(m2 - m1)
    idx_ref[...] = jnp.where(lane == 0, i1, jnp.where(lane == 1, i2, 0))
    wt_ref[...] = jnp.where(lane == 0, 1.0 / (1.0 + e2), jnp.where(lane == 1, e2 / (1.0 + e2), 0.0))


def router(h, g, w_router):
    t, d = h.shape
    e = w_router.shape[1]
    tm = _pick(t, 512)
    wr = jnp.zeros((d, LANES), F32).at[:, :e].set(w_router.astype(F32))
    row = lambda i: (i, 0)
    return pl.pallas_call(
        functools.partial(_router_kernel, n_experts=e),
        grid=(t // tm,),
        in_specs=[pl.BlockSpec((tm, d), row), pl.BlockSpec((1, d), lambda i: (0, 0)),
                  pl.BlockSpec((d, LANES), lambda i: (0, 0))],
        out_specs=[pl.BlockSpec((tm, d // 2), row), pl.BlockSpec((tm, LANES), row), pl.BlockSpec((tm, LANES), row)],
        out_shape=[jax.ShapeDtypeStruct((t, d // 2), jnp.uint32), jax.ShapeDtypeStruct((t, LANES), jnp.int32),
                   jax.ShapeDtypeStruct((t, LANES), F32)],
        compiler_params=_cparams("parallel"),
        name="moe_router",
    )(h, g.reshape(1, d).astype(F32), wr)


def _row_copy(src_hbm, dst_ref, sem, src_row, dst_row):
    return pltpu.make_async_copy(src_hbm.at[pl.ds(src_row, 1), :], dst_ref.at[pl.ds(dst_row, 1), :], sem)


def _gather_kernel(tok_ref, x_hbm, o_ref, buf_ref, sem):
    rows = o_ref.shape[0]

    def start(r2, c):
        for par in range(2):
            r = 2 * r2 + par
            _row_copy(x_hbm, buf_ref, sem, tok_ref[0, 0, r], r).start(priority=par)
        return c

    def wait(r, c):
        _row_copy(x_hbm, buf_ref, sem, 0, r).wait()
        return c

    lax.fori_loop(0, rows // 2, start, 0)
    lax.fori_loop(0, rows, wait, 0)
    o_ref[...] = _unpack_bf16_pairs(buf_ref[...]).astype(o_ref.dtype)


def gather_rows(x, row_tok, tm, out_dtype):
    n_rows = row_tok.shape[0]
    d = 2 * x.shape[1]
    return pl.pallas_call(
        _gather_kernel,
        grid=(n_rows // tm,),
        in_specs=[pl.BlockSpec((1, 1, tm), lambda i: (i, 0, 0), memory_space=pltpu.SMEM),
                  pl.BlockSpec(memory_space=pl.ANY)],
        out_specs=pl.BlockSpec((tm, d), lambda i: (i, 0)),
        out_shape=jax.ShapeDtypeStruct((n_rows, d), out_dtype),
        scratch_shapes=[pltpu.VMEM((tm, d // 2), x.dtype), pltpu.SemaphoreType.DMA(())],
        compiler_params=_cparams("arbitrary"),
        name="moe_gather",
    )(row_tok.reshape(n_rows // tm, 1, tm), x)


def _combine_kernel(p0_ref, p1_ref, y_hbm, wt_ref, h_ref, o_ref, a_ref, b_ref, sem, *, tile_words):
    rows = o_ref.shape[0]

    def start(r, c):
        _row_copy(y_hbm, a_ref, sem.at[0], p0_ref[0, 0, r], r).start(priority=0)
        _row_copy(y_hbm, b_ref, sem.at[1], p1_ref[0, 0, r], r).start(priority=1)
        return c

    def wait(r, c):
        _row_copy(y_hbm, a_ref, sem.at[0], 0, r).wait()
        _row_copy(y_hbm, b_ref, sem.at[1], 0, r).wait()
        return c

    lax.fori_loop(0, rows, start, 0)
    lax.fori_loop(0, rows, wait, 0)
    wt = wt_ref[...]

    def unpack(ref):
        return jnp.concatenate([_unpack_bf16_pairs(ref[:, c:c + tile_words])
                                for c in range(0, ref.shape[1], tile_words)], axis=1)

    o_ref[...] = h_ref[...] + wt[:, 0:1] * unpack(a_ref) + wt[:, 1:2] * unpack(b_ref)


def combine_rows(y_rows, pos0, pos1, wt, h, tm, tile_words):
    t, d = h.shape
    idx_spec = pl.BlockSpec((1, 1, tm), lambda i: (i, 0, 0), memory_space=pltpu.SMEM)
    return pl.pallas_call(
        functools.partial(_combine_kernel, tile_words=tile_words),
        grid=(t // tm,),
        in_specs=[idx_spec, idx_spec, pl.BlockSpec(memory_space=pl.ANY),
                  pl.BlockSpec((tm, LANES), lambda i: (i, 0)), pl.BlockSpec((tm, d), lambda i: (i, 0))],
        out_specs=pl.BlockSpec((tm, d), lambda i: (i, 0)),
        out_shape=jax.ShapeDtypeStruct((t, d), F32),
        scratch_shapes=[pltpu.VMEM((tm, d // 2), jnp.uint32), pltpu.VMEM((tm, d // 2), jnp.uint32),
                        pltpu.SemaphoreType.DMA((2,))],
        compiler_params=_cparams("arbitrary"),
        name="moe_combine",
    )(pos0.reshape(t // tm, 1, tm), pos1.reshape(t // tm, 1, tm), y_rows, wt, h)


def moe_layer(h, g, w_router, wg, wu, wd, layer, tm_pref=512):
    t, d = h.shape
    n_exp = w_router.shape[1]
    tm = _pick(t, tm_pref)
    xn, idx, wt = router(h, g, w_router)
    e_flat = idx[:, :TOP_K].reshape(-1)
    onehot = (e_flat[:, None] == jnp.arange(n_exp)[None, :]).astype(jnp.int32)
    csum = jnp.cumsum(onehot, axis=0)
    rank = jnp.sum((csum - onehot) * onehot, axis=1)
    counts = csum[-1]
    padded = (counts + tm - 1) // tm * tm
    pad_end = jnp.cumsum(padded)
    dest = (pad_end - padded)[e_flat] + rank
    n_rows = t * TOP_K + n_exp * tm
    row_tok = jnp.zeros((n_rows,), jnp.int32).at[dest].set(jnp.arange(t * TOP_K, dtype=jnp.int32) // TOP_K)
    n_blk = n_rows // tm
    blk = jnp.arange(n_blk)
    n_used = (pad_end[-1:] // tm).astype(jnp.int32)
    blk_expert = jnp.minimum(jnp.searchsorted(pad_end, blk * tm, side='right'), n_exp - 1).astype(jnp.int32)
    blk_expert = jnp.where(blk < n_used[0], blk_expert, blk_expert[n_used[0] - 1])
    first = ((blk == 0) | (blk_expert != jnp.roll(blk_expert, 1))).astype(jnp.int32)
    groups = (layer * n_exp + blk_expert, first, n_used)

    xs = gather_rows(xn, row_tok, tm, BF16)
    hid = grouped_glu(xs, wg, wu, groups, tm)
    tn_down = _pick(d, 512)
    y_rows = grouped_matmul([hid], wd, groups, F32, tm, tn_down, pack_out=True)
    pos = dest.reshape(t, TOP_K).astype(jnp.int32)
    return combine_rows(y_rows, pos[:, 0], pos[:, 1], wt, h, tm, tn_down // 2)


def _even_in_proj_weight(w_in):
    d = w_in.shape[0]
    g0 = A_Q + 6 * A_KVW
    gates = w_in[:, g0:g0 + 3 * A_HEADS].reshape(d, 3, A_KV, A_GRP)
    blocks = [w_in[:, :g0], w_in[:, g0 + 3 * A_HEADS:]]
    for k in range(A_KV):
        gk = gates[:, :, k, :].reshape(d, 3 * A_GRP)
        blocks.append(jnp.pad(gk, ((0, 0), (0, LANES - 3 * A_GRP))))
    w = jnp.concatenate(blocks, axis=1)
    return jnp.pad(w, ((0, 0), (0, ZE_WIDTH - w.shape[1])))


def kernel(x, p, rel_bias, norm_mix, norm_ffn, norm_ple, norm_final, w_in_e, cmp_pos, cmp_w1, cmp_w2, conv_w,
           w_out_e, w_gate_d, w_up_d, w_down_d, w_qkv_o, sinks, w_out_o, w_router, w_gate_m, w_up_m, w_down_m,
           w_ple, w_ple_gate):
    b, s, d = x.shape
    t = b * s
    depth = norm_mix.shape[0]
    tm = _pick(t, 1024)
    tm_down = _pick(t, 512)

    tw, tc = nsa_bias_tiles(rel_bias, s)
    tb = bias_table(swa_bucket_idx(), rel_bias, A_HEADS, C_HEADS)

    w_in_relaid = jnp.stack([_even_in_proj_weight(w_in_e[j]) for j in range(w_in_e.shape[0])])
    w_gate_m = w_gate_m.reshape((-1,) + w_gate_m.shape[2:])
    w_up_m = w_up_m.reshape((-1,) + w_up_m.shape[2:])
    w_down_m = w_down_m.reshape((-1,) + w_down_m.shape[2:])
    p = p.reshape(depth, t, -1)

    h = x.reshape(t, d).astype(F32)
    for i in range(depth):
        j = i // 2
        layer = _dense_groups(t // tm, j)
        hn = rmsnorm(h, norm_mix[i], BF16)
        if i % 2 == 0:
            ze = grouped_matmul([hn], w_in_relaid, layer, BF16, tm, 1024).reshape(b, s, ZE_WIDTH)
            kvcm = compress(ze, cmp_pos[j], cmp_w1[j], cmp_w2[j])
            att = nsa_attention(ze, kvcm, tw, tc).reshape(t, A_Q)
            cnv = short_conv(ze, conv_w[j]).reshape(t, CONV_DIM)
            h = grouped_matmul([att, cnv], w_out_e, layer, F32, tm, 512, res=h)
            hn = rmsnorm(h, norm_ffn[i], BF16)
            hid = grouped_glu(hn, w_gate_d, w_up_d, layer, tm)
            h = grouped_matmul([hid], w_down_d, _dense_groups(t // tm_down, j), F32, tm_down, 512, res=h)
        else:
            zo = grouped_matmul([hn], w_qkv_o, layer, BF16, tm, 1280).reshape(b, s, -1)
            att = swa_attention(zo, sinks[j], tb).reshape(t, C_Q)
            h = grouped_matmul([att], w_out_o, layer, F32, tm, 512, res=h)
            h = moe_layer(h, norm_ffn[i], w_router[j], w_gate_m, w_up_m, w_down_m, j)
        hn = rmsnorm(h, norm_ple[i], BF16)
        h = ple(hn, p, w_ple_gate, w_ple, i, h)
    return rmsnorm(h, norm_final, x.dtype).reshape(b, s, d)
```

```python
import functools
import math

import numpy as np
import jax
import jax.numpy as jnp
from jax import lax
from jax.experimental import pallas as pl
from jax.experimental.pallas import tpu as pltpu

F32 = jnp.float32
BF16 = jnp.bfloat16

LANES = 128
VMEM_LIMIT_BYTES = 56 * 1024 * 1024
VMEM_LIMIT_GLU_BYTES = 60 * 1024 * 1024

EPS = 1e-6
A_HEADS = 8
A_KV = 2
A_GRP = A_HEADS // A_KV
A_DH = 128
A_Q = A_HEADS * A_DH
A_KVW = A_KV * A_DH
CMP_LEN = 32
CMP_STRIDE = 16
SLC_BLK = 64
SLC_TOPK = 16
WIN_A = 512
CONV_DIM = 1024
CONV_K = 3
C_HEADS = 32
C_KV = 4
C_GRP = C_HEADS // C_KV
C_DH = 64
C_Q = C_HEADS * C_DH
C_KVW = C_KV * C_DH
WIN_C = 128
N_BUCKETS = 32
MAX_DIST = 1024
TOP_K = 2

QBLK = 128
FAR_REL = 8
NEG = -1e30
BIG = 1e30

ZE_Q = 0
ZE_KC, ZE_VC, ZE_KS, ZE_VS, ZE_KW, ZE_VW = 8, 10, 12, 14, 16, 18
ZE_GB, ZE_GC, ZE_HC = 20, 28, 36
ZE_GATE = 44
ZE_WIDTH = 48 * LANES


def _cparams(*sem, vmem_limit_bytes=VMEM_LIMIT_BYTES):
    return pltpu.CompilerParams(dimension_semantics=sem, vmem_limit_bytes=vmem_limit_bytes)


def _sigmoid(v):
    return 1.0 / (1.0 + jnp.exp(-v))


def _pick(n, pref):
    t = min(n, pref)
    while n % t:
        t -= LANES if t > LANES else 8
    return t


def _pack_bf16_pairs(x):
    half = x.shape[1] // 2
    lo = lax.bitcast_convert_type(x[:, :half].astype(BF16).astype(F32), jnp.uint32)
    hi = lax.bitcast_convert_type(x[:, half:].astype(BF16).astype(F32), jnp.uint32)
    return lax.shift_right_logical(lo, jnp.uint32(16)) | hi


def _unpack_bf16_pairs(w):
    lo = lax.bitcast_convert_type(lax.shift_left(w, jnp.uint32(16)), F32)
    hi = lax.bitcast_convert_type(w & jnp.uint32(0xFFFF0000), F32)
    return jnp.concatenate([lo, hi], axis=1)


def _rmsnorm_kernel(x_ref, g_ref, o_ref):
    x = x_ref[...]
    y = x * lax.rsqrt(jnp.mean(x * x, axis=-1, keepdims=True) + EPS)
    o_ref[...] = (y * g_ref[...]).astype(o_ref.dtype)


def rmsnorm(x, g, out_dtype):
    t, d = x.shape
    tm = _pick(t, 512)
    return pl.pallas_call(
        _rmsnorm_kernel,
        grid=(t // tm,),
        in_specs=[pl.BlockSpec((tm, d), lambda i: (i, 0)), pl.BlockSpec((1, d), lambda i: (0, 0))],
        out_specs=pl.BlockSpec((tm, d), lambda i: (i, 0)),
        out_shape=jax.ShapeDtypeStruct((t, d), out_dtype),
        compiler_params=_cparams("parallel"),
        name="rmsnorm",
    )(x, g.reshape(1, d).astype(F32))


def _dense_groups(n_blk, layer):
    grp = jnp.full((n_blk,), layer, jnp.int32)
    first = (jnp.arange(n_blk) == 0).astype(jnp.int32)
    return grp, first, jnp.full((1,), n_blk, jnp.int32)


def _gmm_kernel(grp_ref, first_ref, nu_ref, *refs, k_sizes, has_res, pack_out):
    n_x = len(k_sizes)
    w_ref = refs[n_x]
    o_ref, wb_ref = refs[-2], refs[-1]
    i = pl.program_id(1)

    @pl.when(first_ref[i] == 1)
    def _():
        wb_ref[...] = w_ref[...].astype(BF16)

    @pl.when(i < nu_ref[0])
    def _():
        acc = None
        off = 0
        for x_ref, k in zip(refs[:n_x], k_sizes):
            part = jnp.dot(x_ref[...], wb_ref[off:off + k, :], preferred_element_type=F32)
            acc = part if acc is None else acc + part
            off += k
        if has_res:
            acc = refs[n_x + 1][...] + acc
        o_ref[...] = _pack_bf16_pairs(acc) if pack_out else acc.astype(o_ref.dtype)

    @pl.when(i >= nu_ref[0])
    def _():
        o_ref[...] = jnp.zeros(o_ref.shape, o_ref.dtype)


def grouped_matmul(xs, w, groups, out_dtype, tm, tn_pref, res=None, pack_out=False):
    t = xs[0].shape[0]
    kw, n = w.shape[1], w.shape[2]
    k_sizes = tuple(x.shape[1] for x in xs)
    tn = _pick(n, tn_pref)
    in_specs = [pl.BlockSpec((tm, k), lambda j, i, g, f, nu: (i, 0)) for k in k_sizes]
    in_specs.append(pl.BlockSpec((None, kw, tn), lambda j, i, g, f, nu: (g[i], 0, j)))
    args = list(xs) + [w]
    if res is not None:
        in_specs.append(pl.BlockSpec((tm, tn), lambda j, i, g, f, nu: (i, j)))
        args.append(res)
    grid_spec = pltpu.PrefetchScalarGridSpec(
        num_scalar_prefetch=3,
        grid=(n // tn, t // tm),
        in_specs=in_specs,
        out_specs=pl.BlockSpec((tm, tn // 2 if pack_out else tn), lambda j, i, g, f, nu: (i, j)),
        scratch_shapes=[pltpu.VMEM((kw, tn), BF16)],
    )
    out_shape = jax.ShapeDtypeStruct((t, n // 2), jnp.uint32) if pack_out else jax.ShapeDtypeStruct((t, n), out_dtype)
    return pl.pallas_call(
        functools.partial(_gmm_kernel, k_sizes=k_sizes, has_res=res is not None, pack_out=pack_out),
        grid_spec=grid_spec,
        out_shape=out_shape,
        compiler_params=_cparams("arbitrary", "arbitrary"),
        name="matmul",
    )(*groups, *args)


def _glu_kernel(grp_ref, first_ref, nu_ref, nxt_ref, x_ref, wg_hbm, wu_hbm, o_ref, stage_ref, wb_ref, sem):
    j = pl.program_id(0)
    i = pl.program_id(1)
    tf = o_ref.shape[1]

    def tile_copies(grp, col_tile):
        col = pl.multiple_of(col_tile * tf, LANES)
        return [pltpu.make_async_copy(w.at[grp, :, pl.ds(col, tf)], stage_ref.at[k], sem.at[k])
                for k, w in enumerate((wg_hbm, wu_hbm))]

    @pl.when(first_ref[i] == 1)
    def _():
        @pl.when((i == 0) & (j == 0))
        def _():
            for cp in tile_copies(grp_ref[0], 0):
                cp.start()

        for cp in tile_copies(grp_ref[i], j):
            cp.wait()
        for k in range(2):
            wb_ref[k] = stage_ref[k].astype(BF16)
        nxt = nxt_ref[i]

        @pl.when(nxt >= 0)
        def _():
            for cp in tile_copies(nxt, j):
                cp.start()

        @pl.when((nxt < 0) & (j + 1 < pl.num_programs(0)))
        def _():
            for cp in tile_copies(grp_ref[0], j + 1):
                cp.start()

    @pl.when(i < nu_ref[0])
    def _():
        x = x_ref[...]
        g = jnp.dot(x, wb_ref[0], preferred_element_type=F32)
        u = jnp.dot(x, wb_ref[1], preferred_element_type=F32)
        o_ref[...] = (g * _sigmoid(g) * u).astype(o_ref.dtype)

    @pl.when(i >= nu_ref[0])
    def _():
        o_ref[...] = jnp.zeros(o_ref.shape, o_ref.dtype)


def _next_group(groups):
    grp, first, n_used = groups
    n_blk = grp.shape[0]
    blk = jnp.arange(n_blk)
    starts = jnp.where((first == 1) & (blk < n_used[0]), blk, n_blk)
    after = jnp.concatenate([jnp.flip(lax.cummin(jnp.flip(starts)))[1:], jnp.full((1,), n_blk, starts.dtype)])
    return jnp.where(after < n_blk, grp[jnp.minimum(after, n_blk - 1)], -1).astype(jnp.int32)


def grouped_glu(x, wg, wu, groups, tm, tf_pref=1408):
    t, d = x.shape
    f = wg.shape[2]
    tf = _pick(f, tf_pref)
    grid_spec = pltpu.PrefetchScalarGridSpec(
        num_scalar_prefetch=4,
        grid=(f // tf, t // tm),
        in_specs=[pl.BlockSpec((tm, d), lambda j, i, g, fl, nu, nx: (i, 0)),
                  pl.BlockSpec(memory_space=pl.ANY), pl.BlockSpec(memory_space=pl.ANY)],
        out_specs=pl.BlockSpec((tm, tf), lambda j, i, g, fl, nu, nx: (i, j)),
        scratch_shapes=[pltpu.VMEM((2, d, tf), F32), pltpu.VMEM((2, d, tf), BF16), pltpu.SemaphoreType.DMA((2,))],
    )
    return pl.pallas_call(
        _glu_kernel,
        grid_spec=grid_spec,
        out_shape=jax.ShapeDtypeStruct((t, f), BF16),
        compiler_params=_cparams("arbitrary", "arbitrary", vmem_limit_bytes=VMEM_LIMIT_GLU_BYTES),
        name="glu",
    )(*groups, _next_group(groups), x, wg, wu)


def _ple_kernel(hn_ref, p_ref, wg_ref, wp_ref, h_ref, o_ref, wgb_ref, wpb_ref):
    @pl.when(pl.program_id(1) == 0)
    def _():
        wgb_ref[...] = wg_ref[...].astype(BF16)
        wpb_ref[...] = wp_ref[...].astype(BF16)

    gate = _sigmoid(jnp.dot(hn_ref[...], wgb_ref[...], preferred_element_type=F32))
    pe = jnp.dot(p_ref[...].astype(BF16), wpb_ref[...], preferred_element_type=F32)
    o_ref[...] = h_ref[...] + gate * pe


def ple(hn, p, wg, wp, layer, h, tm_pref=512, tn_pref=1024):
    t, d = h.shape
    pd = p.shape[2]
    tm, tn = _pick(t, tm_pref), _pick(d, tn_pref)
    return pl.pallas_call(
        _ple_kernel,
        grid=(d // tn, t // tm),
        in_specs=[
            pl.BlockSpec((tm, d), lambda j, i: (i, 0)),
            pl.BlockSpec((None, tm, pd), lambda j, i: (layer, i, 0)),
            pl.BlockSpec((None, d, tn), lambda j, i: (layer, 0, j)),
            pl.BlockSpec((None, pd, tn), lambda j, i: (layer, 0, j)),
            pl.BlockSpec((tm, tn), lambda j, i: (i, j)),
        ],
        out_specs=pl.BlockSpec((tm, tn), lambda j, i: (i, j)),
        out_shape=jax.ShapeDtypeStruct((t, d), F32),
        scratch_shapes=[pltpu.VMEM((d, tn), BF16), pltpu.VMEM((pd, tn), BF16)],
        compiler_params=_cparams("arbitrary", "arbitrary"),
        name="ple",
    )(hn, p, wg, wp, h)


def t5_bucket(dist):
    n = jnp.maximum(jnp.asarray(dist, jnp.int32), 0)
    exact = N_BUCKETS // 2
    nf = jnp.maximum(n, 1).astype(F32)
    large = exact + (jnp.log(nf / exact) / math.log(MAX_DIST / exact) * (N_BUCKETS - exact)).astype(jnp.int32)
    return jnp.where(n < exact, n, jnp.minimum(large, N_BUCKETS - 1))


def _bias_table_kernel(idx_ref, tab_ref, o_ref, *, head0):
    h = head0 + pl.program_id(0)
    idx = idx_ref[0]
    val = jnp.full(idx.shape, tab_ref[0, h], F32)
    for b in range(1, N_BUCKETS):
        val = jnp.where(idx == b, tab_ref[b, h], val)
    o_ref[0, 0] = jnp.where(idx < 0, NEG, val)


def bias_table(bucket_idx, rel_bias, head0, n_heads):
    n, r, c = bucket_idx.shape
    return pl.pallas_call(
        functools.partial(_bias_table_kernel, head0=head0),
        grid=(n_heads, n),
        in_specs=[
            pl.BlockSpec((1, r, c), lambda h, i: (i, 0, 0)),
            pl.BlockSpec(memory_space=pltpu.SMEM),
        ],
        out_specs=pl.BlockSpec((1, 1, r, c), lambda h, i: (h, i, 0, 0)),
        out_shape=jax.ShapeDtypeStruct((n_heads, n, r, c), F32),
        compiler_params=_cparams("parallel", "parallel"),
        name="bias_table",
    )(bucket_idx, rel_bias.astype(F32))


def tile_bucket_idx(n_rel):
    a = np.arange(QBLK)[None, :, None]
    c = np.arange(QBLK)[None, None, :]
    r = np.arange(n_rel)[:, None, None]
    return t5_bucket(QBLK * r + a - c)


def cmp_bucket_idx(s, n_cmp_pad):
    t = np.arange(s).reshape(s // QBLK, QBLK, 1)
    cend = (np.arange(n_cmp_pad) * CMP_STRIDE + CMP_LEN - 1)[None, None, :]
    return t5_bucket(t - cend)


def _compress_kernel(t_ref, pos_ref, w1_ref, w2_ref, o_ref, tf_ref, *, n_pad):
    s = t_ref.shape[1]
    half = CMP_LEN // 2
    tf_ref[0:s, :] = t_ref[0].astype(F32)
    tf_ref[s:s + half, :] = jnp.zeros((half, A_DH), F32)
    hid = jnp.zeros((n_pad, w1_ref.shape[2]), F32)
    for l in range(CMP_LEN):
        rows = tf_ref[pl.ds(l, n_pad, stride=CMP_STRIDE), :] + pos_ref[0, l:l + 1, :]
        hid = hid + jnp.dot(rows.astype(BF16), w1_ref[0, l * A_DH:(l + 1) * A_DH, :], preferred_element_type=F32)
    act = jax.nn.gelu(hid)
    out = jnp.dot(act.astype(BF16), w2_ref[0], preferred_element_type=F32)
    valid = lax.broadcasted_iota(jnp.int32, out.shape, 0) < n_pad - 1
    o_ref[0, 0, 0] = jnp.where(valid, out, 0.0).astype(o_ref.dtype)


def compress(ze, cmp_pos, cmp_w1, cmp_w2):
    b, s, _ = ze.shape
    n_pad = s // CMP_STRIDE
    hid = cmp_w1.shape[2]
    return pl.pallas_call(
        functools.partial(_compress_kernel, n_pad=n_pad),
        grid=(2, b, A_KV),
        in_specs=[
            pl.BlockSpec((1, s, A_DH), lambda w, bi, k: (bi, 0, ZE_KC + 2 * w + k)),
            pl.BlockSpec((1, CMP_LEN, A_DH), lambda w, bi, k: (w, 0, 0)),
            pl.BlockSpec((1, CMP_LEN * A_DH, hid), lambda w, bi, k: (w, 0, 0)),
            pl.BlockSpec((1, hid, A_DH), lambda w, bi, k: (w, 0, 0)),
        ],
        out_specs=pl.BlockSpec((1, 1, 1, n_pad, A_DH), lambda w, bi, k: (w, bi, k, 0, 0)),
        out_shape=jax.ShapeDtypeStruct((2, b, A_KV, n_pad, A_DH), BF16),
        scratch_shapes=[pltpu.VMEM((s + CMP_LEN // 2, A_DH), F32)],
        compiler_params=_cparams("parallel", "parallel", "parallel"),
        name="nsa_compress",
    )(ze, cmp_pos.astype(F32), cmp_w1.astype(BF16), cmp_w2.astype(BF16))


def _nsa_kernel(q_ref, kcm_ref, vcm_ref, ks_ref, vs_ref, kw_ref, vw_ref, g_ref, tw_ref, tc_ref, o_ref,
                qs_ref, vct_ref, vst_ref, vwt_ref, sel_ref, m_ref, l_ref, acc_ref):
    i = pl.program_id(2)
    tq = QBLK
    hq = A_GRP * tq
    s_len = ks_ref.shape[1]
    n_cmp = kcm_ref.shape[3]
    n_slc = s_len // SLC_BLK
    nt = (((1,), (1,)), ((), ()))
    heads = [slice(g * tq, (g + 1) * tq) for g in range(A_GRP)]

    def per_head(x):
        return jnp.concatenate([x] * A_GRP, axis=1)

    @pl.when(i == 0)
    def _():
        vct_ref[...] = vcm_ref[0, 0, 0].astype(F32).T.astype(BF16)

        def transpose_tile(t, c):
            off = pl.multiple_of(t * tq, tq)
            vst_ref[:, pl.ds(off, tq)] = vs_ref[0, pl.ds(off, tq), :].astype(F32).T.astype(BF16)
            vwt_ref[:, pl.ds(off, tq)] = vw_ref[0, pl.ds(off, tq), :].astype(F32).T.astype(BF16)
            return c

        lax.fori_loop(0, s_len // tq, transpose_tile, 0)

    q = q_ref[0]
    for g in range(A_GRP):
        qs_ref[heads[g], :] = (q[:, g * A_DH:(g + 1) * A_DH].astype(F32) * (A_DH ** -0.5)).astype(BF16)
    t_row = i * tq + lax.broadcasted_iota(jnp.int32, (1, tq), 1)

    n_col = lax.broadcasted_iota(jnp.int32, (n_cmp, hq), 0)
    t_all = i * tq + (lax.broadcasted_iota(jnp.int32, (n_cmp, hq), 1) & (tq - 1))
    mask_c = (t_all >= n_col * CMP_STRIDE + (CMP_LEN - 1)) & (n_col < n_cmp - 1)
    s_c = lax.dot_general(kcm_ref[0, 0, 0], qs_ref[...], nt, preferred_element_type=F32) + tc_ref[0, 0]
    s_c = jnp.where(mask_c, s_c, NEG)
    e_c = jnp.where(mask_c, jnp.exp(s_c - jnp.max(s_c, axis=0, keepdims=True)), 0.0)
    r_c = 1.0 / jnp.maximum(jnp.sum(e_c, axis=0, keepdims=True), 1e-30)
    o_c = jnp.dot(vct_ref[...], e_c.astype(BF16), preferred_element_type=F32) * r_c
    p_c = e_c * r_c
    p_sum = (p_c[:, heads[0]] + p_c[:, heads[1]]) + (p_c[:, heads[2]] + p_c[:, heads[3]])
    ov_j = lax.broadcasted_iota(jnp.int32, (LANES, n_cmp), 0) * SLC_BLK
    ov_n = lax.broadcasted_iota(jnp.int32, (LANES, n_cmp), 1) * CMP_STRIDE
    overlap_t = ((ov_n < ov_j + SLC_BLK) & (ov_n + CMP_LEN > ov_j)).astype(F32)
    imp_t = jnp.dot(overlap_t, p_sum, preferred_element_type=F32, precision=lax.Precision.HIGHEST)

    jb = lax.broadcasted_iota(jnp.int32, (LANES, tq), 0)
    cur = t_row // SLC_BLK
    imp_t = jnp.where(jb * SLC_BLK > t_row, -BIG, imp_t)
    imp_t = jnp.where((jb == 0) | (jb == cur) | (jb == cur - 1), BIG, imp_t)
    sub = 8
    slabs = [imp_t[r * sub:(r + 1) * sub] for r in range(n_slc // sub)]
    jb_slab = lax.broadcasted_iota(jnp.int32, (sub, tq), 0)
    ranks = [jnp.zeros((sub, tq), F32) for _ in slabs]
    for j2 in range(n_slc):
        other = imp_t[j2:j2 + 1, :]
        for r, v in enumerate(slabs):
            if r * sub > j2:
                ahead = other >= v
            elif r * sub + sub - 1 <= j2:
                ahead = other > v
            else:
                ahead = (other > v) | ((other == v) & (jb_slab + r * sub > j2))
            ranks[r] = ranks[r] + jnp.where(ahead, 1.0, 0.0)
    sel_ref[...] = jnp.concatenate([jnp.where(rk < float(min(SLC_TOPK, n_slc)), 1.0, 0.0) for rk in ranks], axis=0)

    n_w = WIN_A // tq + 1
    jw0 = jnp.maximum(i - WIN_A // tq, 0)
    w_start = pl.multiple_of(jw0 * tq, tq)
    dist_w = t_row - (w_start + lax.broadcasted_iota(jnp.int32, (n_w * tq, tq), 0))
    madd_w = jnp.where((dist_w >= 0) & (dist_w < WIN_A), 0.0, NEG)
    s_w = lax.dot_general(kw_ref[0, pl.ds(w_start, n_w * tq), :], qs_ref[...], nt, preferred_element_type=F32)
    s_w = s_w + jnp.concatenate([tw_ref[0, jnp.clip(i - jw0 - cb, 0, FAR_REL)] for cb in range(n_w)], axis=0)
    s_w = s_w + per_head(madd_w)
    e_w = jnp.exp(s_w - jnp.max(s_w, axis=0, keepdims=True))
    r_w = 1.0 / jnp.maximum(jnp.sum(e_w, axis=0, keepdims=True), 1e-30)
    o_w = jnp.dot(vwt_ref[:, pl.ds(w_start, n_w * tq)], e_w.astype(BF16), preferred_element_type=F32) * r_w

    ck = 4 * tq
    blocks_per_chunk = ck // SLC_BLK
    m_ref[...] = jnp.full(m_ref.shape, NEG, F32)
    l_ref[...] = jnp.zeros(l_ref.shape, F32)
    acc_ref[...] = jnp.zeros(acc_ref.shape, F32)
    key_iota = lax.broadcasted_iota(jnp.int32, (ck, tq), 0)

    def chunk(c, carry):
        start = pl.multiple_of(c * ck, ck)
        sel_rows = sel_ref[pl.ds(pl.multiple_of(c * blocks_per_chunk, blocks_per_chunk), blocks_per_chunk), :]
        chosen = jnp.concatenate([jnp.broadcast_to(sel_rows[b:b + 1, :], (SLC_BLK, tq))
                                  for b in range(blocks_per_chunk)], axis=0)
        madd = jnp.where(jnp.where(start + key_iota <= t_row, chosen, 0.0) > 0.5, 0.0, NEG)
        s_s = lax.dot_general(ks_ref[0, pl.ds(start, ck), :], qs_ref[...], nt, preferred_element_type=F32)
        s_s = s_s + jnp.concatenate([tw_ref[0, jnp.clip(i - (ck // tq) * c - cb, 0, FAR_REL)]
                                     for cb in range(ck // tq)], axis=0)
        s_s = s_s + per_head(madd)
        m_old = m_ref[...]
        m_new = jnp.maximum(m_old, jnp.max(s_s, axis=0, keepdims=True))
        p = jnp.exp(s_s - m_new)
        alpha = jnp.exp(m_old - m_new)
        l_ref[...] = alpha * l_ref[...] + jnp.sum(p, axis=0, keepdims=True)
        acc_ref[...] = alpha * acc_ref[...] + jnp.dot(vst_ref[:, pl.ds(start, ck)], p.astype(BF16),
                                                      preferred_element_type=F32)
        m_ref[...] = m_new
        return carry

    lax.fori_loop(0, (i * tq) // ck + 1, chunk, 0)
    o_s = acc_ref[...] / jnp.maximum(l_ref[...], 1e-30)

    gate_t = _sigmoid(g_ref[0].astype(F32)).T
    outs = []
    for g in range(A_GRP):
        o_t = (gate_t[g:g + 1, :] * o_c[:, heads[g]]
               + gate_t[A_GRP + g:A_GRP + g + 1, :] * o_s[:, heads[g]]
               + gate_t[2 * A_GRP + g:2 * A_GRP + g + 1, :] * o_w[:, heads[g]])
        outs.append(o_t.T)
    o_ref[0] = jnp.concatenate(outs, axis=-1).astype(o_ref.dtype)


def nsa_attention(ze, kvcm, tw, tc):
    b, s, _ = ze.shape
    n_cmp = kvcm.shape[3]
    hq = A_GRP * QBLK

    def kv_spec(col):
        return pl.BlockSpec((1, s, A_DH), lambda bi, k, i: (bi, 0, col + k))

    return pl.pallas_call(
        _nsa_kernel,
        grid=(b, A_KV, s // QBLK),
        in_specs=[
            pl.BlockSpec((1, QBLK, A_GRP * A_DH), lambda bi, k, i: (bi, i, k)),
            pl.BlockSpec((1, 1, 1, n_cmp, A_DH), lambda bi, k, i: (0, bi, k, 0, 0)),
            pl.BlockSpec((1, 1, 1, n_cmp, A_DH), lambda bi, k, i: (1, bi, k, 0, 0)),
            kv_spec(ZE_KS), kv_spec(ZE_VS), kv_spec(ZE_KW), kv_spec(ZE_VW),
            pl.BlockSpec((1, QBLK, LANES), lambda bi, k, i: (bi, i, ZE_GATE + k)),
            pl.BlockSpec((1, FAR_REL + 1, QBLK, hq), lambda bi, k, i: (k, 0, 0, 0)),
            pl.BlockSpec((1, 1, n_cmp, hq), lambda bi, k, i: (k, i, 0, 0)),
        ],
        out_specs=pl.BlockSpec((1, QBLK, A_GRP * A_DH), lambda bi, k, i: (bi, i, k)),
        out_shape=jax.ShapeDtypeStruct((b, s, A_Q), BF16),
        scratch_shapes=[pltpu.VMEM((hq, A_DH), BF16), pltpu.VMEM((A_DH, n_cmp), BF16),
                        pltpu.VMEM((A_DH, s), BF16), pltpu.VMEM((A_DH, s), BF16),
                        pltpu.VMEM((s // SLC_BLK, QBLK), F32),
                        pltpu.VMEM((1, hq), F32), pltpu.VMEM((1, hq), F32), pltpu.VMEM((A_DH, hq), F32)],
        compiler_params=_cparams("parallel", "parallel", "arbitrary"),
        name="nsa_attention",
    )(ze, kvcm, kvcm, ze, ze, ze, ze, ze, tw, tc)


def nsa_bias_tiles(rel_bias, s):
    n_cmp = s // CMP_STRIDE
    tw = bias_table(jnp.swapaxes(tile_bucket_idx(FAR_REL + 1), 1, 2), rel_bias, 0, A_HEADS)
    tw = tw.reshape(A_KV, A_GRP, FAR_REL + 1, QBLK, QBLK).transpose(0, 2, 3, 1, 4)
    tc = bias_table(jnp.swapaxes(cmp_bucket_idx(s, n_cmp), 1, 2), rel_bias, 0, A_HEADS)
    tc = tc.reshape(A_KV, A_GRP, s // QBLK, n_cmp, QBLK).transpose(0, 2, 3, 1, 4)
    return (tw.reshape(A_KV, FAR_REL + 1, QBLK, A_GRP * QBLK), tc.reshape(A_KV, s // QBLK, n_cmp, A_GRP * QBLK))


def _conv_kernel(gb_ref, gc_ref, hc_ref, gcp_ref, hcp_ref, w_ref, o_ref):
    i = pl.program_id(1)
    u = gc_ref[0].astype(F32) * hc_ref[0].astype(F32)
    prev = gcp_ref[0].astype(F32) * hcp_ref[0].astype(F32)
    prev = jnp.where(i > 0, prev, 0.0)
    n_prev = prev.shape[0]
    p1 = prev[n_prev - 1:n_prev, :]
    p2 = prev[n_prev - 2:n_prev - 1, :]
    row = lax.broadcasted_iota(jnp.int32, u.shape, 0)
    u1 = jnp.where(row == 0, p1, pltpu.roll(u, 1, 0))
    u2 = jnp.where(row == 0, p2, jnp.where(row == 1, p1, pltpu.roll(u, 2, 0)))
    w = w_ref[...]
    y = w[0:1, :] * u2 + w[1:2, :] * u1 + w[2:3, :] * u
    o_ref[0] = (gb_ref[0].astype(F32) * y).astype(o_ref.dtype)


def short_conv(ze, conv_w, ts_pref=512, halo=16):
    b, s, _ = ze.shape
    ts = _pick(s, ts_pref)
    cw = 4 * LANES
    nc = CONV_DIM // cw

    def cur(col):
        return pl.BlockSpec((1, ts, cw), lambda bi, i, c: (bi, i, col * LANES // cw + c))

    def prev(col):
        return pl.BlockSpec((1, halo, cw), lambda bi, i, c: (bi, jnp.maximum(i * (ts // halo) - 1, 0), col * LANES // cw + c))

    return pl.pallas_call(
        _conv_kernel,
        grid=(b, s // ts, nc),
        in_specs=[cur(ZE_GB), cur(ZE_GC), cur(ZE_HC), prev(ZE_GC), prev(ZE_HC),
                  pl.BlockSpec((CONV_K, cw), lambda bi, i, c: (0, c))],
        out_specs=pl.BlockSpec((1, ts, cw), lambda bi, i, c: (bi, i, c)),
        out_shape=jax.ShapeDtypeStruct((b, s, CONV_DIM), BF16),
        compiler_params=_cparams("parallel", "parallel", "parallel"),
        name="short_conv",
    )(ze, ze, ze, ze, ze, conv_w.astype(F32))


def _swa_kernel(q_ref, kp_ref, kc_ref, vp_ref, vc_ref, sink_ref, tb_ref, o_ref):
    i = pl.program_id(1)
    tq = QBLK
    half = LANES // 2
    nt = (((1,), (1,)), ((), ()))
    q = q_ref[0]
    k_all = jnp.concatenate([kp_ref[0], kc_ref[0]], axis=0).astype(F32) * (C_DH ** -0.5)
    v_all = jnp.concatenate([vp_ref[0], vc_ref[0]], axis=0).astype(F32)
    in_lo = lax.broadcasted_iota(jnp.int32, (2 * tq, LANES), 1) < half
    row_lo = lax.broadcasted_iota(jnp.int32, (LANES, tq), 0) < half
    ones = jnp.ones((LANES, 2 * tq), BF16)
    prev_tile = jnp.where(i > 0, 1, 2)
    pairs_per_kv = C_GRP // 2
    blocks = []
    for kv in range(C_KV):
        lanes = slice((kv // 2) * LANES, (kv // 2 + 1) * LANES)
        own = in_lo if kv % 2 == 0 else ~in_lo
        k_own = jnp.where(own, k_all[:, lanes], 0.0)
        v_own = jnp.where(own, v_all[:, lanes], 0.0)
        k_par = [k_own, pltpu.roll(k_own, half, 1)]
        v_par = [v_own, pltpu.roll(v_own, half, 1)]
        if kv % 2:
            k_par.reverse()
            v_par.reverse()
        q2 = jnp.concatenate([q[:, (kv * pairs_per_kv + m) * LANES:(kv * pairs_per_kv + m + 1) * LANES]
                              for m in range(pairs_per_kv)], axis=0)
        res_t = []
        for par in range(2):
            hs = [kv * C_GRP + 2 * m + par for m in range(pairs_per_kv)]
            s_t = lax.dot_general(k_par[par].astype(BF16), q2, nt, preferred_element_type=F32)
            bias_t = jnp.concatenate([jnp.concatenate([tb_ref[h, prev_tile], tb_ref[h, 0]], axis=0) for h in hs], axis=1)
            sink = jnp.concatenate([jnp.full((1, tq), sink_ref[h], F32) for h in hs], axis=1)
            s_t = s_t + bias_t
            m_col = jnp.maximum(jnp.max(s_t, axis=0, keepdims=True), sink)
            e_t = jnp.exp(s_t - m_col).astype(BF16)
            vo = jnp.concatenate([v_par[par].T.astype(BF16), ones], axis=0)
            nd = jnp.dot(vo, e_t, preferred_element_type=F32)
            res_t.append(nd[:LANES] / (nd[LANES:] + jnp.exp(sink - m_col)))
        both = jnp.where(jnp.concatenate([row_lo] * pairs_per_kv, axis=1), res_t[0], res_t[1])
        blocks.extend(both[:, m * tq:(m + 1) * tq].T for m in range(pairs_per_kv))
    o_ref[0] = jnp.concatenate(blocks, axis=-1).astype(o_ref.dtype)


def swa_bucket_idx():
    a = np.arange(QBLK)[:, None]
    c = np.arange(QBLK)[None, :]
    own = jnp.where(a - c >= 0, t5_bucket(a - c), -1).T
    before = jnp.where(QBLK + a - c < WIN_C, t5_bucket(QBLK + a - c), -1).T
    return jnp.stack([own, before, jnp.full((QBLK, QBLK), -1, jnp.int32)]).astype(jnp.int32)


def swa_attention(zo, sinks, tb):
    b, s, _ = zo.shape
    kcol = C_Q // C_KVW
    prev = lambda bi, i: jnp.maximum(i - 1, 0)
    return pl.pallas_call(
        _swa_kernel,
        grid=(b, s // QBLK),
        in_specs=[
            pl.BlockSpec((1, QBLK, C_Q), lambda bi, i: (bi, i, 0)),
            pl.BlockSpec((1, QBLK, C_KVW), lambda bi, i: (bi, prev(bi, i), kcol)),
            pl.BlockSpec((1, QBLK, C_KVW), lambda bi, i: (bi, i, kcol)),
            pl.BlockSpec((1, QBLK, C_KVW), lambda bi, i: (bi, prev(bi, i), kcol + 1)),
            pl.BlockSpec((1, QBLK, C_KVW), lambda bi, i: (bi, i, kcol + 1)),
            pl.BlockSpec(memory_space=pltpu.SMEM),
            pl.BlockSpec((C_HEADS, 3, QBLK, QBLK), lambda bi, i: (0, 0, 0, 0)),
        ],
        out_specs=pl.BlockSpec((1, QBLK, C_Q), lambda bi, i: (bi, i, 0)),
        out_shape=jax.ShapeDtypeStruct((b, s, C_Q), BF16),
        compiler_params=_cparams("parallel", "parallel"),
        name="swa_attention",
    )(zo, zo, zo, zo, zo, sinks.astype(F32), tb)


def _router_kernel(h_ref, g_ref, wr_ref, xn_ref, idx_ref, wt_ref, *, n_experts):
    x = h_ref[...]
    xn = x * lax.rsqrt(jnp.mean(x * x, axis=-1, keepdims=True) + EPS) * g_ref[...]
    packed = _pack_bf16_pairs(xn)
    n_tiles = packed.shape[1] // LANES
    for s in range(n_tiles):
        xn_ref[pl.ds(s, x.shape[0], stride=n_tiles), :] = packed[:, s * LANES:(s + 1) * LANES]
    logits = jnp.dot(xn, wr_ref[...], preferred_element_type=F32, precision=lax.Precision.HIGHEST)
    lane = lax.broadcasted_iota(jnp.int32, logits.shape, 1)
    lg = jnp.where(lane < n_experts, logits, NEG)
    m1 = jnp.max(lg, axis=-1, keepdims=True)
    i1 = jnp.min(jnp.where(lg == m1, lane, LANES), axis=-1, keepdims=True)
    lg2 = jnp.where(lane == i1, NEG, lg)
    m2 = jnp.max(lg2, axis=-1, keepdims=True)
    i2 = jnp.min(jnp.where(lg2 == m2, lane, LANES), axis=-1, keepdims=True)
    e2 = jnp.exp(m2 - m1)
    idx_ref[...] = jnp.where(lane == 0, i1, jnp.where(lane == 1, i2, 0))
    wt_ref[...] = jnp.where(lane == 0, 1.0 / (1.0 + e2), jnp.where(lane == 1, e2 / (1.0 + e2), 0.0))


def router(h, g, w_router):
    t, d = h.shape
    e = w_router.shape[1]
    tm = _pick(t, 512)
    wr = jnp.zeros((d, LANES), F32).at[:, :e].set(w_router.astype(F32))
    row = lambda i: (i, 0)
    return pl.pallas_call(
        functools.partial(_router_kernel, n_experts=e),
        grid=(t // tm,),
        in_specs=[pl.BlockSpec((tm, d), row), pl.BlockSpec((1, d), lambda i: (0, 0)),
                  pl.BlockSpec((d, LANES), lambda i: (0, 0))],
        out_specs=[pl.BlockSpec((tm * (d // 2 // LANES), LANES), row), pl.BlockSpec((tm, LANES), row),
                   pl.BlockSpec((tm, LANES), row)],
        out_shape=[jax.ShapeDtypeStruct((t * (d // 2 // LANES), LANES), jnp.uint32), jax.ShapeDtypeStruct((t, LANES), jnp.int32),
                   jax.ShapeDtypeStruct((t, LANES), F32)],
        compiler_params=_cparams("parallel"),
        name="moe_router",
    )(h, g.reshape(1, d).astype(F32), wr)


def _row_copy(src_hbm, dst_ref, sem, src_row, dst_row):
    return pltpu.make_async_copy(src_hbm.at[pl.ds(src_row, 1), :], dst_ref.at[pl.ds(dst_row, 1), :], sem)


def _token_copy(src_hbm, dst_ref, sem, src_tok, dst_tok, n_tiles):
    return pltpu.make_async_copy(src_hbm.at[pl.ds(pl.multiple_of(src_tok * n_tiles, n_tiles), n_tiles), :],
                                 dst_ref.at[pl.ds(pl.multiple_of(dst_tok * n_tiles, n_tiles), n_tiles), :], sem)


def _gather_kernel(tok_ref, x_hbm, o_ref, buf_ref, sem):
    rows = o_ref.shape[0]
    n_tiles = buf_ref.shape[0] // rows

    def start(r2, c):
        for par in range(2):
            r = 2 * r2 + par
            _token_copy(x_hbm, buf_ref, sem, tok_ref[0, 0, r], r, n_tiles).start(priority=par)
        return c

    def wait(r, c):
        _token_copy(x_hbm, buf_ref, sem, 0, r, n_tiles).wait()
        return c

    lax.fori_loop(0, rows // 2, start, 0)
    lax.fori_loop(0, rows, wait, 0)
    packed = jnp.concatenate([buf_ref[pl.ds(s, rows, stride=n_tiles), :] for s in range(n_tiles)], axis=1)
    o_ref[...] = _unpack_bf16_pairs(packed).astype(o_ref.dtype)


def gather_rows(x, row_tok, tm, d, out_dtype):
    n_rows = row_tok.shape[0]
    n_tiles = d // 2 // LANES
    return pl.pallas_call(
        _gather_kernel,
        grid=(n_rows // tm,),
        in_specs=[pl.BlockSpec((1, 1, tm), lambda i: (i, 0, 0), memory_space=pltpu.SMEM),
                  pl.BlockSpec(memory_space=pl.ANY)],
        out_specs=pl.BlockSpec((tm, d), lambda i: (i, 0)),
        out_shape=jax.ShapeDtypeStruct((n_rows, d), out_dtype),
        scratch_shapes=[pltpu.VMEM((tm * n_tiles, LANES), x.dtype), pltpu.SemaphoreType.DMA(())],
        compiler_params=_cparams("arbitrary"),
        name="moe_gather",
    )(row_tok.reshape(n_rows // tm, 1, tm), x)


def _combine_kernel(p0_ref, p1_ref, y_hbm, wt_ref, h_ref, g_ref, o_ref, on_ref, a_ref, b_ref, sem, *, tile_words):
    rows = o_ref.shape[0]

    def start(r, c):
        _row_copy(y_hbm, a_ref, sem.at[0], p0_ref[0, 0, r], r).start(priority=0)
        _row_copy(y_hbm, b_ref, sem.at[1], p1_ref[0, 0, r], r).start(priority=1)
        return c

    def wait(r, c):
        _row_copy(y_hbm, a_ref, sem.at[0], 0, r).wait()
        _row_copy(y_hbm, b_ref, sem.at[1], 0, r).wait()
        return c

    lax.fori_loop(0, rows, start, 0)
    lax.fori_loop(0, rows, wait, 0)
    wt = wt_ref[...]

    def unpack(ref):
        return jnp.concatenate([_unpack_bf16_pairs(ref[:, c:c + tile_words])
                                for c in range(0, ref.shape[1], tile_words)], axis=1)

    out = h_ref[...] + wt[:, 0:1] * unpack(a_ref) + wt[:, 1:2] * unpack(b_ref)
    o_ref[...] = out
    normed = out * lax.rsqrt(jnp.mean(out * out, axis=-1, keepdims=True) + EPS)
    on_ref[...] = (normed * g_ref[...]).astype(on_ref.dtype)


def combine_rows(y_rows, pos0, pos1, wt, h, g_next, tm, tile_words):
    t, d = h.shape
    idx_spec = pl.BlockSpec((1, 1, tm), lambda i: (i, 0, 0), memory_space=pltpu.SMEM)
    return pl.pallas_call(
        functools.partial(_combine_kernel, tile_words=tile_words),
        grid=(t // tm,),
        in_specs=[idx_spec, idx_spec, pl.BlockSpec(memory_space=pl.ANY),
                  pl.BlockSpec((tm, LANES), lambda i: (i, 0)), pl.BlockSpec((tm, d), lambda i: (i, 0)),
                  pl.BlockSpec((1, d), lambda i: (0, 0))],
        out_specs=[pl.BlockSpec((tm, d), lambda i: (i, 0)), pl.BlockSpec((tm, d), lambda i: (i, 0))],
        out_shape=[jax.ShapeDtypeStruct((t, d), F32), jax.ShapeDtypeStruct((t, d), BF16)],
        scratch_shapes=[pltpu.VMEM((tm, d // 2), jnp.uint32), pltpu.VMEM((tm, d // 2), jnp.uint32),
                        pltpu.SemaphoreType.DMA((2,))],
        compiler_params=_cparams("arbitrary"),
        name="moe_combine",
    )(pos0.reshape(t // tm, 1, tm), pos1.reshape(t // tm, 1, tm), y_rows, wt, h, g_next.reshape(1, d).astype(F32))


def moe_layer(h, g, g_next, w_router, wg, wu, wd, layer, tm_pref=512):
    t, d = h.shape
    n_exp = w_router.shape[1]
    tm = _pick(t, tm_pref)
    xn, idx, wt = router(h, g, w_router)
    e_flat = idx[:, :TOP_K].reshape(-1)
    onehot = (e_flat[:, None] == jnp.arange(n_exp)[None, :]).astype(jnp.int32)
    csum = jnp.cumsum(onehot, axis=0)
    rank = jnp.sum((csum - onehot) * onehot, axis=1)
    counts = csum[-1]
    padded = (counts + tm - 1) // tm * tm
    pad_end = jnp.cumsum(padded)
    dest = (pad_end - padded)[e_flat] + rank
    n_rows = t * TOP_K + n_exp * tm
    row_tok = jnp.zeros((n_rows,), jnp.int32).at[dest].set(jnp.arange(t * TOP_K, dtype=jnp.int32) // TOP_K)
    n_blk = n_rows // tm
    blk = jnp.arange(n_blk)
    n_used = (pad_end[-1:] // tm).astype(jnp.int32)
    blk_expert = jnp.minimum(jnp.searchsorted(pad_end, blk * tm, side='right'), n_exp - 1).astype(jnp.int32)
    blk_expert = jnp.where(blk < n_used[0], blk_expert, blk_expert[n_used[0] - 1])
    first = ((blk == 0) | (blk_expert != jnp.roll(blk_expert, 1))).astype(jnp.int32)
    groups = (layer * n_exp + blk_expert, first, n_used)

    xs = gather_rows(xn, row_tok, tm, d, BF16)
    hid = grouped_glu(xs, wg, wu, groups, tm)
    tn_down = _pick(d, 512)
    y_rows = grouped_matmul([hid], wd, groups, F32, tm, tn_down, pack_out=True)
    pos = dest.reshape(t, TOP_K).astype(jnp.int32)
    return combine_rows(y_rows, pos[:, 0], pos[:, 1], wt, h, g_next, tm, tn_down // 2)


def _even_in_proj_weight(w_in):
    d = w_in.shape[0]
    g0 = A_Q + 6 * A_KVW
    gates = w_in[:, g0:g0 + 3 * A_HEADS].reshape(d, 3, A_KV, A_GRP)
    blocks = [w_in[:, :g0], w_in[:, g0 + 3 * A_HEADS:]]
    for k in range(A_KV):
        gk = gates[:, :, k, :].reshape(d, 3 * A_GRP)
        blocks.append(jnp.pad(gk, ((0, 0), (0, LANES - 3 * A_GRP))))
    w = jnp.concatenate(blocks, axis=1)
    return jnp.pad(w, ((0, 0), (0, ZE_WIDTH - w.shape[1])))


def kernel(x, p, rel_bias, norm_mix, norm_ffn, norm_ple, norm_final, w_in_e, cmp_pos, cmp_w1, cmp_w2, conv_w,
           w_out_e, w_gate_d, w_up_d, w_down_d, w_qkv_o, sinks, w_out_o, w_router, w_gate_m, w_up_m, w_down_m,
           w_ple, w_ple_gate):
    b, s, d = x.shape
    t = b * s
    depth = norm_mix.shape[0]
    tm = _pick(t, 1024)
    tm_down = _pick(t, 512)

    tw, tc = nsa_bias_tiles(rel_bias, s)
    tb = bias_table(swa_bucket_idx(), rel_bias, A_HEADS, C_HEADS)

    w_in_relaid = jnp.stack([_even_in_proj_weight(w_in_e[j]) for j in range(w_in_e.shape[0])])
    w_gate_m = w_gate_m.reshape((-1,) + w_gate_m.shape[2:])
    w_up_m = w_up_m.reshape((-1,) + w_up_m.shape[2:])
    w_down_m = w_down_m.reshape((-1,) + w_down_m.shape[2:])
    p = p.reshape(depth, t, -1)

    h = x.reshape(t, d).astype(F32)
    for i in range(depth):
        j = i // 2
        layer = _dense_groups(t // tm, j)
        hn = rmsnorm(h, norm_mix[i], BF16)
        if i % 2 == 0:
            ze = grouped_matmul([hn], w_in_relaid, layer, BF16, tm, 1024).reshape(b, s, ZE_WIDTH)
            kvcm = compress(ze, cmp_pos[j], cmp_w1[j], cmp_w2[j])
            att = nsa_attention(ze, kvcm, tw, tc).reshape(t, A_Q)
            cnv = short_conv(ze, conv_w[j]).reshape(t, CONV_DIM)
            h = grouped_matmul([att, cnv], w_out_e, layer, F32, tm, 512, res=h)
            hn = rmsnorm(h, norm_ffn[i], BF16)
            hid = grouped_glu(hn, w_gate_d, w_up_d, _dense_groups(t // tm_down, j), tm_down)
            h = grouped_matmul([hid], w_down_d, _dense_groups(t // tm_down, j), F32, tm_down, 512, res=h)
        else:
            zo = grouped_matmul([hn], w_qkv_o, layer, BF16, tm, 1280).reshape(b, s, -1)
            att = swa_attention(zo, sinks[j], tb).reshape(t, C_Q)
            h = grouped_matmul([att], w_out_o, layer, F32, tm, 512, res=h)
            h, hn = moe_layer(h, norm_ffn[i], norm_ple[i], w_router[j], w_gate_m, w_up_m, w_down_m, j)
        if i % 2 == 0:
            hn = rmsnorm(h, norm_ple[i], BF16)
        h = ple(hn, p, w_ple_gate, w_ple, i, h)
    return rmsnorm(h, norm_final, x.dtype).reshape(b, s, d)
```

```python
import functools
import math

import numpy as np
import jax
import jax.numpy as jnp
from jax import lax
from jax.experimental import pallas as pl
from jax.experimental.pallas import tpu as pltpu

F32 = jnp.float32
BF16 = jnp.bfloat16

LANES = 128
VMEM_LIMIT_BYTES = 56 * 1024 * 1024
VMEM_LIMIT_STAGED_BYTES = 60 * 1024 * 1024

EPS = 1e-6
A_HEADS = 8
A_KV = 2
A_GRP = A_HEADS // A_KV
A_DH = 128
A_Q = A_HEADS * A_DH
A_KVW = A_KV * A_DH
CMP_LEN = 32
CMP_STRIDE = 16
SLC_BLK = 64
SLC_TOPK = 16
WIN_A = 512
CONV_DIM = 1024
CONV_K = 3
C_HEADS = 32
C_KV = 4
C_GRP = C_HEADS // C_KV
C_DH = 64
C_Q = C_HEADS * C_DH
C_KVW = C_KV * C_DH
WIN_C = 128
N_BUCKETS = 32
MAX_DIST = 1024
TOP_K = 2

QBLK = 128
FAR_REL = 8
NEG = -1e30
BIG = 1e30

ZE_Q = 0
ZE_KC, ZE_VC, ZE_KS, ZE_VS, ZE_KW, ZE_VW = 8, 10, 12, 14, 16, 18
ZE_GB, ZE_GC, ZE_HC = 20, 28, 36
ZE_GATE = 44
ZE_WIDTH = 48 * LANES


def _cparams(*sem, vmem_limit_bytes=VMEM_LIMIT_BYTES):
    return pltpu.CompilerParams(dimension_semantics=sem, vmem_limit_bytes=vmem_limit_bytes)


def _sigmoid(v):
    return 1.0 / (1.0 + jnp.exp(-v))


def _pick(n, pref):
    t = min(n, pref)
    while n % t:
        t -= LANES if t > LANES else 8
    return t


def _pack_bf16_pairs(x):
    half = x.shape[1] // 2
    lo = lax.bitcast_convert_type(x[:, :half].astype(BF16).astype(F32), jnp.uint32)
    hi = lax.bitcast_convert_type(x[:, half:].astype(BF16).astype(F32), jnp.uint32)
    return lax.shift_right_logical(lo, jnp.uint32(16)) | hi


def _unpack_bf16_pairs(w):
    lo = lax.bitcast_convert_type(lax.shift_left(w, jnp.uint32(16)), F32)
    hi = lax.bitcast_convert_type(w & jnp.uint32(0xFFFF0000), F32)
    return jnp.concatenate([lo, hi], axis=1)


def _rmsnorm_kernel(x_ref, g_ref, o_ref):
    x = x_ref[...]
    y = x * lax.rsqrt(jnp.mean(x * x, axis=-1, keepdims=True) + EPS)
    o_ref[...] = (y * g_ref[...]).astype(o_ref.dtype)


def rmsnorm(x, g, out_dtype):
    t, d = x.shape
    tm = _pick(t, 512)
    return pl.pallas_call(
        _rmsnorm_kernel,
        grid=(t // tm,),
        in_specs=[pl.BlockSpec((tm, d), lambda i: (i, 0)), pl.BlockSpec((1, d), lambda i: (0, 0))],
        out_specs=pl.BlockSpec((tm, d), lambda i: (i, 0)),
        out_shape=jax.ShapeDtypeStruct((t, d), out_dtype),
        compiler_params=_cparams("parallel"),
        name="rmsnorm",
    )(x, g.reshape(1, d).astype(F32))


def _dense_groups(n_blk, layer):
    grp = jnp.full((n_blk,), layer, jnp.int32)
    first = (jnp.arange(n_blk) == 0).astype(jnp.int32)
    return grp, first, jnp.full((1,), n_blk, jnp.int32)


def _next_group(groups):
    grp, first, n_used = groups
    n_blk = grp.shape[0]
    blk = jnp.arange(n_blk)
    starts = jnp.where((first == 1) & (blk < n_used[0]), blk, n_blk)
    after = jnp.concatenate([jnp.flip(lax.cummin(jnp.flip(starts)))[1:], jnp.full((1,), n_blk, starts.dtype)])
    return jnp.where(after < n_blk, grp[jnp.minimum(after, n_blk - 1)], -1).astype(jnp.int32)


def _stage_weights(grp_ref, first_ref, nxt_ref, w_hbms, stage_ref, wb_ref, sem, tn):
    j = pl.program_id(0)
    i = pl.program_id(1)

    def tile_copies(grp, col_tile):
        col = pl.multiple_of(col_tile * tn, LANES)
        return [pltpu.make_async_copy(w.at[grp, :, pl.ds(col, tn)], stage_ref.at[k], sem.at[k])
                for k, w in enumerate(w_hbms)]

    @pl.when(first_ref[i] == 1)
    def _():
        @pl.when((i == 0) & (j == 0))
        def _():
            for cp in tile_copies(grp_ref[0], 0):
                cp.start()

        for cp in tile_copies(grp_ref[i], j):
            cp.wait()
        for k in range(len(w_hbms)):
            wb_ref[k] = stage_ref[k].astype(BF16)
        nxt = nxt_ref[i]

        @pl.when(nxt >= 0)
        def _():
            for cp in tile_copies(nxt, j):
                cp.start()

        @pl.when((nxt < 0) & (j + 1 < pl.num_programs(0)))
        def _():
            for cp in tile_copies(grp_ref[0], j + 1):
                cp.start()


def _gmm_kernel(grp_ref, first_ref, nu_ref, nxt_ref, *refs, k_sizes, has_res, has_norm, pack_out):
    n_x = len(k_sizes)
    w_hbm = refs[n_x]
    stage_ref, wb_ref, sem = refs[-3:]
    outs = refs[n_x + 1 + has_res + has_norm:-3]
    o_ref = outs[0]
    i = pl.program_id(1)
    _stage_weights(grp_ref, first_ref, nxt_ref, [w_hbm], stage_ref, wb_ref, sem, wb_ref.shape[2])

    @pl.when(i < nu_ref[0])
    def _():
        acc = None
        off = 0
        for x_ref, k in zip(refs[:n_x], k_sizes):
            part = jnp.dot(x_ref[...], wb_ref[0, off:off + k, :], preferred_element_type=F32)
            acc = part if acc is None else acc + part
            off += k
        if has_res:
            acc = refs[n_x + 1][...] + acc
        o_ref[...] = _pack_bf16_pairs(acc) if pack_out else acc.astype(o_ref.dtype)
        if has_norm:
            normed = acc * lax.rsqrt(jnp.mean(acc * acc, axis=-1, keepdims=True) + EPS)
            outs[1][...] = (normed * refs[n_x + 1 + has_res][...]).astype(outs[1].dtype)

    @pl.when(i >= nu_ref[0])
    def _():
        for o in outs:
            o[...] = jnp.zeros(o.shape, o.dtype)


def grouped_matmul(xs, w, groups, out_dtype, tm, tn_pref, res=None, norm_gain=None, pack_out=False):
    t = xs[0].shape[0]
    kw, n = w.shape[1], w.shape[2]
    k_sizes = tuple(x.shape[1] for x in xs)
    tn = _pick(n, tn_pref)
    assert norm_gain is None or tn == n
    idx = lambda j, i, g, f, nu, nx: (i, j)
    in_specs = [pl.BlockSpec((tm, k), lambda j, i, g, f, nu, nx: (i, 0)) for k in k_sizes]
    in_specs.append(pl.BlockSpec(memory_space=pl.ANY))
    args = list(xs) + [w]
    if res is not None:
        in_specs.append(pl.BlockSpec((tm, tn), idx))
        args.append(res)
    if norm_gain is not None:
        in_specs.append(pl.BlockSpec((1, tn), lambda j, i, g, f, nu, nx: (0, 0)))
        args.append(norm_gain.reshape(1, n).astype(F32))
    out_specs = [pl.BlockSpec((tm, tn // 2 if pack_out else tn), idx)]
    out_shape = [jax.ShapeDtypeStruct((t, n // 2), jnp.uint32) if pack_out else jax.ShapeDtypeStruct((t, n), out_dtype)]
    if norm_gain is not None:
        out_specs.append(pl.BlockSpec((tm, tn), idx))
        out_shape.append(jax.ShapeDtypeStruct((t, n), BF16))
    grid_spec = pltpu.PrefetchScalarGridSpec(
        num_scalar_prefetch=4,
        grid=(n // tn, t // tm),
        in_specs=in_specs,
        out_specs=out_specs,
        scratch_shapes=[pltpu.VMEM((1, kw, tn), F32), pltpu.VMEM((1, kw, tn), BF16), pltpu.SemaphoreType.DMA((1,))],
    )
    out = pl.pallas_call(
        functools.partial(_gmm_kernel, k_sizes=k_sizes, has_res=res is not None, has_norm=norm_gain is not None,
                          pack_out=pack_out),
        grid_spec=grid_spec,
        out_shape=out_shape,
        compiler_params=_cparams("arbitrary", "arbitrary", vmem_limit_bytes=VMEM_LIMIT_STAGED_BYTES),
        name="matmul",
    )(*groups, _next_group(groups), *args)
    return out if norm_gain is not None else out[0]


def _glu_kernel(grp_ref, first_ref, nu_ref, nxt_ref, x_ref, wg_hbm, wu_hbm, o_ref, stage_ref, wb_ref, sem):
    i = pl.program_id(1)
    _stage_weights(grp_ref, first_ref, nxt_ref, [wg_hbm, wu_hbm], stage_ref, wb_ref, sem, o_ref.shape[1])

    @pl.when(i < nu_ref[0])
    def _():
        x = x_ref[...]
        g = jnp.dot(x, wb_ref[0], preferred_element_type=F32)
        u = jnp.dot(x, wb_ref[1], preferred_element_type=F32)
        o_ref[...] = (g * _sigmoid(g) * u).astype(o_ref.dtype)

    @pl.when(i >= nu_ref[0])
    def _():
        o_ref[...] = jnp.zeros(o_ref.shape, o_ref.dtype)


def grouped_glu(x, wg, wu, groups, tm, tf_pref=1408):
    t, d = x.shape
    f = wg.shape[2]
    tf = _pick(f, tf_pref)
    grid_spec = pltpu.PrefetchScalarGridSpec(
        num_scalar_prefetch=4,
        grid=(f // tf, t // tm),
        in_specs=[pl.BlockSpec((tm, d), lambda j, i, g, fl, nu, nx: (i, 0)),
                  pl.BlockSpec(memory_space=pl.ANY), pl.BlockSpec(memory_space=pl.ANY)],
        out_specs=pl.BlockSpec((tm, tf), lambda j, i, g, fl, nu, nx: (i, j)),
        scratch_shapes=[pltpu.VMEM((2, d, tf), F32), pltpu.VMEM((2, d, tf), BF16), pltpu.SemaphoreType.DMA((2,))],
    )
    return pl.pallas_call(
        _glu_kernel,
        grid_spec=grid_spec,
        out_shape=jax.ShapeDtypeStruct((t, f), BF16),
        compiler_params=_cparams("arbitrary", "arbitrary", vmem_limit_bytes=VMEM_LIMIT_STAGED_BYTES),
        name="glu",
    )(*groups, _next_group(groups), x, wg, wu)


def _ple_kernel(hn_ref, p_ref, wg_ref, wp_ref, h_ref, o_ref, wgb_ref, wpb_ref):
    @pl.when(pl.program_id(1) == 0)
    def _():
        wgb_ref[...] = wg_ref[...].astype(BF16)
        wpb_ref[...] = wp_ref[...].astype(BF16)

    gate = _sigmoid(jnp.dot(hn_ref[...], wgb_ref[...], preferred_element_type=F32))
    pe = jnp.dot(p_ref[...].astype(BF16), wpb_ref[...], preferred_element_type=F32)
    o_ref[...] = h_ref[...] + gate * pe


def ple(hn, p, wg, wp, layer, h, tm_pref=512, tn_pref=1024):
    t, d = h.shape
    pd = p.shape[2]
    tm, tn = _pick(t, tm_pref), _pick(d, tn_pref)
    return pl.pallas_call(
        _ple_kernel,
        grid=(d // tn, t // tm),
        in_specs=[
            pl.BlockSpec((tm, d), lambda j, i: (i, 0)),
            pl.BlockSpec((None, tm, pd), lambda j, i: (layer, i, 0)),
            pl.BlockSpec((None, d, tn), lambda j, i: (layer, 0, j)),
            pl.BlockSpec((None, pd, tn), lambda j, i: (layer, 0, j)),
            pl.BlockSpec((tm, tn), lambda j, i: (i, j)),
        ],
        out_specs=pl.BlockSpec((tm, tn), lambda j, i: (i, j)),
        out_shape=jax.ShapeDtypeStruct((t, d), F32),
        scratch_shapes=[pltpu.VMEM((d, tn), BF16), pltpu.VMEM((pd, tn), BF16)],
        compiler_params=_cparams("arbitrary", "arbitrary"),
        name="ple",
    )(hn, p, wg, wp, h)


def t5_bucket(dist):
    n = jnp.maximum(jnp.asarray(dist, jnp.int32), 0)
    exact = N_BUCKETS // 2
    nf = jnp.maximum(n, 1).astype(F32)
    large = exact + (jnp.log(nf / exact) / math.log(MAX_DIST / exact) * (N_BUCKETS - exact)).astype(jnp.int32)
    return jnp.where(n < exact, n, jnp.minimum(large, N_BUCKETS - 1))


def _bias_table_kernel(idx_ref, tab_ref, o_ref, *, head0):
    h = head0 + pl.program_id(0)
    idx = idx_ref[0]
    val = jnp.full(idx.shape, tab_ref[0, h], F32)
    for b in range(1, N_BUCKETS):
        val = jnp.where(idx == b, tab_ref[b, h], val)
    o_ref[0, 0] = jnp.where(idx < 0, NEG, val)


def bias_table(bucket_idx, rel_bias, head0, n_heads):
    n, r, c = bucket_idx.shape
    return pl.pallas_call(
        functools.partial(_bias_table_kernel, head0=head0),
        grid=(n_heads, n),
        in_specs=[
            pl.BlockSpec((1, r, c), lambda h, i: (i, 0, 0)),
            pl.BlockSpec(memory_space=pltpu.SMEM),
        ],
        out_specs=pl.BlockSpec((1, 1, r, c), lambda h, i: (h, i, 0, 0)),
        out_shape=jax.ShapeDtypeStruct((n_heads, n, r, c), F32),
        compiler_params=_cparams("parallel", "parallel"),
        name="bias_table",
    )(bucket_idx, rel_bias.astype(F32))


def tile_bucket_idx(n_rel):
    a = np.arange(QBLK)[None, :, None]
    c = np.arange(QBLK)[None, None, :]
    r = np.arange(n_rel)[:, None, None]
    return t5_bucket(QBLK * r + a - c)


def cmp_bucket_idx(s, n_cmp_pad):
    t = np.arange(s).reshape(s // QBLK, QBLK, 1)
    cend = (np.arange(n_cmp_pad) * CMP_STRIDE + CMP_LEN - 1)[None, None, :]
    return t5_bucket(t - cend)


def _compress_kernel(t_ref, pos_ref, w1_ref, w2_ref, o_ref, tf_ref, *, n_pad):
    s = t_ref.shape[1]
    half = CMP_LEN // 2
    tf_ref[0:s, :] = t_ref[0].astype(F32)
    tf_ref[s:s + half, :] = jnp.zeros((half, A_DH), F32)
    hid = jnp.zeros((n_pad, w1_ref.shape[2]), F32)
    for l in range(CMP_LEN):
        rows = tf_ref[pl.ds(l, n_pad, stride=CMP_STRIDE), :] + pos_ref[0, l:l + 1, :]
        hid = hid + jnp.dot(rows.astype(BF16), w1_ref[0, l * A_DH:(l + 1) * A_DH, :], preferred_element_type=F32)
    act = jax.nn.gelu(hid)
    out = jnp.dot(act.astype(BF16), w2_ref[0], preferred_element_type=F32)
    valid = lax.broadcasted_iota(jnp.int32, out.shape, 0) < n_pad - 1
    o_ref[0, 0, 0] = jnp.where(valid, out, 0.0).astype(o_ref.dtype)


def compress(ze, cmp_pos, cmp_w1, cmp_w2):
    b, s, _ = ze.shape
    n_pad = s // CMP_STRIDE
    hid = cmp_w1.shape[2]
    return pl.pallas_call(
        functools.partial(_compress_kernel, n_pad=n_pad),
        grid=(2, b, A_KV),
        in_specs=[
            pl.BlockSpec((1, s, A_DH), lambda w, bi, k: (bi, 0, ZE_KC + 2 * w + k)),
            pl.BlockSpec((1, CMP_LEN, A_DH), lambda w, bi, k: (w, 0, 0)),
            pl.BlockSpec((1, CMP_LEN * A_DH, hid), lambda w, bi, k: (w, 0, 0)),
            pl.BlockSpec((1, hid, A_DH), lambda w, bi, k: (w, 0, 0)),
        ],
        out_specs=pl.BlockSpec((1, 1, 1, n_pad, A_DH), lambda w, bi, k: (w, bi, k, 0, 0)),
        out_shape=jax.ShapeDtypeStruct((2, b, A_KV, n_pad, A_DH), BF16),
        scratch_shapes=[pltpu.VMEM((s + CMP_LEN // 2, A_DH), F32)],
        compiler_params=_cparams("parallel", "parallel", "parallel"),
        name="nsa_compress",
    )(ze, cmp_pos.astype(F32), cmp_w1.astype(BF16), cmp_w2.astype(BF16))


def _nsa_kernel(q_ref, kcm_ref, vcm_ref, ks_ref, vs_ref, kw_ref, vw_ref, g_ref, tw_ref, tc_ref, o_ref,
                qs_ref, vct_ref, vst_ref, vwt_ref, sel_ref, m_ref, l_ref, acc_ref):
    i = pl.program_id(2)
    tq = QBLK
    hq = A_GRP * tq
    s_len = ks_ref.shape[1]
    n_cmp = kcm_ref.shape[3]
    n_slc = s_len // SLC_BLK
    nt = (((1,), (1,)), ((), ()))
    heads = [slice(g * tq, (g + 1) * tq) for g in range(A_GRP)]

    def per_head(x):
        return jnp.concatenate([x] * A_GRP, axis=1)

    @pl.when(i == 0)
    def _():
        vct_ref[...] = vcm_ref[0, 0, 0].astype(F32).T.astype(BF16)

        def transpose_tile(t, c):
            off = pl.multiple_of(t * tq, tq)
            vst_ref[:, pl.ds(off, tq)] = vs_ref[0, pl.ds(off, tq), :].astype(F32).T.astype(BF16)
            vwt_ref[:, pl.ds(off, tq)] = vw_ref[0, pl.ds(off, tq), :].astype(F32).T.astype(BF16)
            return c

        lax.fori_loop(0, s_len // tq, transpose_tile, 0)

    q = q_ref[0]
    for g in range(A_GRP):
        qs_ref[heads[g], :] = (q[:, g * A_DH:(g + 1) * A_DH].astype(F32) * (A_DH ** -0.5)).astype(BF16)
    t_row = i * tq + lax.broadcasted_iota(jnp.int32, (1, tq), 1)

    n_col = lax.broadcasted_iota(jnp.int32, (n_cmp, hq), 0)
    t_all = i * tq + (lax.broadcasted_iota(jnp.int32, (n_cmp, hq), 1) & (tq - 1))
    mask_c = (t_all >= n_col * CMP_STRIDE + (CMP_LEN - 1)) & (n_col < n_cmp - 1)
    s_c = lax.dot_general(kcm_ref[0, 0, 0], qs_ref[...], nt, preferred_element_type=F32) + tc_ref[0, 0]
    s_c = jnp.where(mask_c, s_c, NEG)
    e_c = jnp.where(mask_c, jnp.exp(s_c - jnp.max(s_c, axis=0, keepdims=True)), 0.0)
    r_c = 1.0 / jnp.maximum(jnp.sum(e_c, axis=0, keepdims=True), 1e-30)
    o_c = jnp.dot(vct_ref[...], e_c.astype(BF16), preferred_element_type=F32) * r_c
    p_c = e_c * r_c
    p_sum = (p_c[:, heads[0]] + p_c[:, heads[1]]) + (p_c[:, heads[2]] + p_c[:, heads[3]])
    ov_j = lax.broadcasted_iota(jnp.int32, (LANES, n_cmp), 0) * SLC_BLK
    ov_n = lax.broadcasted_iota(jnp.int32, (LANES, n_cmp), 1) * CMP_STRIDE
    overlap_t = ((ov_n < ov_j + SLC_BLK) & (ov_n + CMP_LEN > ov_j)).astype(F32)
    imp_t = jnp.dot(overlap_t, p_sum, preferred_element_type=F32, precision=lax.Precision.HIGHEST)

    jb = lax.broadcasted_iota(jnp.int32, (LANES, tq), 0)
    cur = t_row // SLC_BLK
    imp_t = jnp.where(jb * SLC_BLK > t_row, -BIG, imp_t)
    imp_t = jnp.where((jb == 0) | (jb == cur) | (jb == cur - 1), BIG, imp_t)
    sub = 8
    slabs = [imp_t[r * sub:(r + 1) * sub] for r in range(n_slc // sub)]
    jb_slab = lax.broadcasted_iota(jnp.int32, (sub, tq), 0)
    ranks = [jnp.zeros((sub, tq), F32) for _ in slabs]
    for j2 in range(n_slc):
        other = imp_t[j2:j2 + 1, :]
        for r, v in enumerate(slabs):
            if r * sub > j2:
                ahead = other >= v
            elif r * sub + sub - 1 <= j2:
                ahead = other > v
            else:
                ahead = (other > v) | ((other == v) & (jb_slab + r * sub > j2))
            ranks[r] = ranks[r] + jnp.where(ahead, 1.0, 0.0)
    sel_ref[...] = jnp.concatenate([jnp.where(rk < float(min(SLC_TOPK, n_slc)), 1.0, 0.0) for rk in ranks], axis=0)

    n_w = WIN_A // tq + 1
    jw0 = jnp.maximum(i - WIN_A // tq, 0)
    w_start = pl.multiple_of(jw0 * tq, tq)
    dist_w = t_row - (w_start + lax.broadcasted_iota(jnp.int32, (n_w * tq, tq), 0))
    madd_w = jnp.where((dist_w >= 0) & (dist_w < WIN_A), 0.0, NEG)
    s_w = lax.dot_general(kw_ref[0, pl.ds(w_start, n_w * tq), :], qs_ref[...], nt, preferred_element_type=F32)
    s_w = s_w + jnp.concatenate([tw_ref[0, jnp.clip(i - jw0 - cb, 0, FAR_REL)] for cb in range(n_w)], axis=0)
    s_w = s_w + per_head(madd_w)
    e_w = jnp.exp(s_w - jnp.max(s_w, axis=0, keepdims=True))
    r_w = 1.0 / jnp.maximum(jnp.sum(e_w, axis=0, keepdims=True), 1e-30)
    o_w = jnp.dot(vwt_ref[:, pl.ds(w_start, n_w * tq)], e_w.astype(BF16), preferred_element_type=F32) * r_w

    ck = 4 * tq
    blocks_per_chunk = ck // SLC_BLK
    m_ref[...] = jnp.full(m_ref.shape, NEG, F32)
    l_ref[...] = jnp.zeros(l_ref.shape, F32)
    acc_ref[...] = jnp.zeros(acc_ref.shape, F32)
    key_iota = lax.broadcasted_iota(jnp.int32, (ck, tq), 0)

    def chunk(c, carry):
        start = pl.multiple_of(c * ck, ck)
        sel_rows = sel_ref[pl.ds(pl.multiple_of(c * blocks_per_chunk, blocks_per_chunk), blocks_per_chunk), :]
        chosen = jnp.concatenate([jnp.broadcast_to(sel_rows[b:b + 1, :], (SLC_BLK, tq))
                                  for b in range(blocks_per_chunk)], axis=0)
        madd = jnp.where(jnp.where(start + key_iota <= t_row, chosen, 0.0) > 0.5, 0.0, NEG)
        s_s = lax.dot_general(ks_ref[0, pl.ds(start, ck), :], qs_ref[...], nt, preferred_element_type=F32)
        s_s = s_s + jnp.concatenate([tw_ref[0, jnp.clip(i - (ck // tq) * c - cb, 0, FAR_REL)]
                                     for cb in range(ck // tq)], axis=0)
        s_s = s_s + per_head(madd)
        m_old = m_ref[...]
        m_new = jnp.maximum(m_old, jnp.max(s_s, axis=0, keepdims=True))
        p = jnp.exp(s_s - m_new)
        alpha = jnp.exp(m_old - m_new)
        l_ref[...] = alpha * l_ref[...] + jnp.sum(p, axis=0, keepdims=True)
        acc_ref[...] = alpha * acc_ref[...] + jnp.dot(vst_ref[:, pl.ds(start, ck)], p.astype(BF16),
                                                      preferred_element_type=F32)
        m_ref[...] = m_new
        return carry

    lax.fori_loop(0, (i * tq) // ck + 1, chunk, 0)
    o_s = acc_ref[...] / jnp.maximum(l_ref[...], 1e-30)

    gate_t = _sigmoid(g_ref[0].astype(F32)).T
    outs = []
    for g in range(A_GRP):
        o_t = (gate_t[g:g + 1, :] * o_c[:, heads[g]]
               + gate_t[A_GRP + g:A_GRP + g + 1, :] * o_s[:, heads[g]]
               + gate_t[2 * A_GRP + g:2 * A_GRP + g + 1, :] * o_w[:, heads[g]])
        outs.append(o_t.T)
    o_ref[0] = jnp.concatenate(outs, axis=-1).astype(o_ref.dtype)


def nsa_attention(ze, kvcm, tw, tc):
    b, s, _ = ze.shape
    n_cmp = kvcm.shape[3]
    hq = A_GRP * QBLK

    def kv_spec(col):
        return pl.BlockSpec((1, s, A_DH), lambda bi, k, i: (bi, 0, col + k))

    return pl.pallas_call(
        _nsa_kernel,
        grid=(b, A_KV, s // QBLK),
        in_specs=[
            pl.BlockSpec((1, QBLK, A_GRP * A_DH), lambda bi, k, i: (bi, i, k)),
            pl.BlockSpec((1, 1, 1, n_cmp, A_DH), lambda bi, k, i: (0, bi, k, 0, 0)),
            pl.BlockSpec((1, 1, 1, n_cmp, A_DH), lambda bi, k, i: (1, bi, k, 0, 0)),
            kv_spec(ZE_KS), kv_spec(ZE_VS), kv_spec(ZE_KW), kv_spec(ZE_VW),
            pl.BlockSpec((1, QBLK, LANES), lambda bi, k, i: (bi, i, ZE_GATE + k)),
            pl.BlockSpec((1, FAR_REL + 1, QBLK, hq), lambda bi, k, i: (k, 0, 0, 0)),
            pl.BlockSpec((1, 1, n_cmp, hq), lambda bi, k, i: (k, i, 0, 0)),
        ],
        out_specs=pl.BlockSpec((1, QBLK, A_GRP * A_DH), lambda bi, k, i: (bi, i, k)),
        out_shape=jax.ShapeDtypeStruct((b, s, A_Q), BF16),
        scratch_shapes=[pltpu.VMEM((hq, A_DH), BF16), pltpu.VMEM((A_DH, n_cmp), BF16),
                        pltpu.VMEM((A_DH, s), BF16), pltpu.VMEM((A_DH, s), BF16),
                        pltpu.VMEM((s // SLC_BLK, QBLK), F32),
                        pltpu.VMEM((1, hq), F32), pltpu.VMEM((1, hq), F32), pltpu.VMEM((A_DH, hq), F32)],
        compiler_params=_cparams("parallel", "parallel", "arbitrary"),
        name="nsa_attention",
    )(ze, kvcm, kvcm, ze, ze, ze, ze, ze, tw, tc)


def nsa_bias_tiles(rel_bias, s):
    n_cmp = s // CMP_STRIDE
    tw = bias_table(jnp.swapaxes(tile_bucket_idx(FAR_REL + 1), 1, 2), rel_bias, 0, A_HEADS)
    tw = tw.reshape(A_KV, A_GRP, FAR_REL + 1, QBLK, QBLK).transpose(0, 2, 3, 1, 4)
    tc = bias_table(jnp.swapaxes(cmp_bucket_idx(s, n_cmp), 1, 2), rel_bias, 0, A_HEADS)
    tc = tc.reshape(A_KV, A_GRP, s // QBLK, n_cmp, QBLK).transpose(0, 2, 3, 1, 4)
    return (tw.reshape(A_KV, FAR_REL + 1, QBLK, A_GRP * QBLK), tc.reshape(A_KV, s // QBLK, n_cmp, A_GRP * QBLK))


def _conv_kernel(gb_ref, gc_ref, hc_ref, gcp_ref, hcp_ref, w_ref, o_ref):
    i = pl.program_id(1)
    u = gc_ref[0].astype(F32) * hc_ref[0].astype(F32)
    prev = gcp_ref[0].astype(F32) * hcp_ref[0].astype(F32)
    prev = jnp.where(i > 0, prev, 0.0)
    n_prev = prev.shape[0]
    p1 = prev[n_prev - 1:n_prev, :]
    p2 = prev[n_prev - 2:n_prev - 1, :]
    row = lax.broadcasted_iota(jnp.int32, u.shape, 0)
    u1 = jnp.where(row == 0, p1, pltpu.roll(u, 1, 0))
    u2 = jnp.where(row == 0, p2, jnp.where(row == 1, p1, pltpu.roll(u, 2, 0)))
    w = w_ref[...]
    y = w[0:1, :] * u2 + w[1:2, :] * u1 + w[2:3, :] * u
    o_ref[0] = (gb_ref[0].astype(F32) * y).astype(o_ref.dtype)


def short_conv(ze, conv_w, ts_pref=512, halo=16):
    b, s, _ = ze.shape
    ts = _pick(s, ts_pref)
    cw = 4 * LANES
    nc = CONV_DIM // cw

    def cur(col):
        return pl.BlockSpec((1, ts, cw), lambda bi, i, c: (bi, i, col * LANES // cw + c))

    def prev(col):
        return pl.BlockSpec((1, halo, cw), lambda bi, i, c: (bi, jnp.maximum(i * (ts // halo) - 1, 0), col * LANES // cw + c))

    return pl.pallas_call(
        _conv_kernel,
        grid=(b, s // ts, nc),
        in_specs=[cur(ZE_GB), cur(ZE_GC), cur(ZE_HC), prev(ZE_GC), prev(ZE_HC),
                  pl.BlockSpec((CONV_K, cw), lambda bi, i, c: (0, c))],
        out_specs=pl.BlockSpec((1, ts, cw), lambda bi, i, c: (bi, i, c)),
        out_shape=jax.ShapeDtypeStruct((b, s, CONV_DIM), BF16),
        compiler_params=_cparams("parallel", "parallel", "parallel"),
        name="short_conv",
    )(ze, ze, ze, ze, ze, conv_w.astype(F32))


def _swa_kernel(q_ref, kp_ref, kc_ref, vp_ref, vc_ref, sink_ref, tb_ref, o_ref):
    i = pl.program_id(1)
    tq = QBLK
    half = LANES // 2
    nt = (((1,), (1,)), ((), ()))
    q = q_ref[0]
    k_all = jnp.concatenate([kp_ref[0], kc_ref[0]], axis=0).astype(F32) * (C_DH ** -0.5)
    v_all = jnp.concatenate([vp_ref[0], vc_ref[0]], axis=0).astype(F32)
    in_lo = lax.broadcasted_iota(jnp.int32, (2 * tq, LANES), 1) < half
    row_lo = lax.broadcasted_iota(jnp.int32, (LANES, tq), 0) < half
    ones = jnp.ones((LANES, 2 * tq), BF16)
    prev_tile = jnp.where(i > 0, 1, 2)
    pairs_per_kv = C_GRP // 2
    blocks = []
    for kv in range(C_KV):
        lanes = slice((kv // 2) * LANES, (kv // 2 + 1) * LANES)
        own = in_lo if kv % 2 == 0 else ~in_lo
        k_own = jnp.where(own, k_all[:, lanes], 0.0)
        v_own = jnp.where(own, v_all[:, lanes], 0.0)
        k_par = [k_own, pltpu.roll(k_own, half, 1)]
        v_par = [v_own, pltpu.roll(v_own, half, 1)]
        if kv % 2:
            k_par.reverse()
            v_par.reverse()
        q2 = jnp.concatenate([q[:, (kv * pairs_per_kv + m) * LANES:(kv * pairs_per_kv + m + 1) * LANES]
                              for m in range(pairs_per_kv)], axis=0)
        res_t = []
        for par in range(2):
            hs = [kv * C_GRP + 2 * m + par for m in range(pairs_per_kv)]
            s_t = lax.dot_general(k_par[par].astype(BF16), q2, nt, preferred_element_type=F32)
            bias_t = jnp.concatenate([jnp.concatenate([tb_ref[h, prev_tile], tb_ref[h, 0]], axis=0) for h in hs], axis=1)
            sink = jnp.concatenate([jnp.full((1, tq), sink_ref[h], F32) for h in hs], axis=1)
            s_t = s_t + bias_t
            m_col = jnp.maximum(jnp.max(s_t, axis=0, keepdims=True), sink)
            e_t = jnp.exp(s_t - m_col).astype(BF16)
            vo = jnp.concatenate([v_par[par].T.astype(BF16), ones], axis=0)
            nd = jnp.dot(vo, e_t, preferred_element_type=F32)
            res_t.append(nd[:LANES] / (nd[LANES:] + jnp.exp(sink - m_col)))
        both = jnp.where(jnp.concatenate([row_lo] * pairs_per_kv, axis=1), res_t[0], res_t[1])
        blocks.extend(both[:, m * tq:(m + 1) * tq].T for m in range(pairs_per_kv))
    o_ref[0] = jnp.concatenate(blocks, axis=-1).astype(o_ref.dtype)


def swa_bucket_idx():
    a = np.arange(QBLK)[:, None]
    c = np.arange(QBLK)[None, :]
    own = jnp.where(a - c >= 0, t5_bucket(a - c), -1).T
    before = jnp.where(QBLK + a - c < WIN_C, t5_bucket(QBLK + a - c), -1).T
    return jnp.stack([own, before, jnp.full((QBLK, QBLK), -1, jnp.int32)]).astype(jnp.int32)


def swa_attention(zo, sinks, tb):
    b, s, _ = zo.shape
    kcol = C_Q // C_KVW
    prev = lambda bi, i: jnp.maximum(i - 1, 0)
    return pl.pallas_call(
        _swa_kernel,
        grid=(b, s // QBLK),
        in_specs=[
            pl.BlockSpec((1, QBLK, C_Q), lambda bi, i: (bi, i, 0)),
            pl.BlockSpec((1, QBLK, C_KVW), lambda bi, i: (bi, prev(bi, i), kcol)),
            pl.BlockSpec((1, QBLK, C_KVW), lambda bi, i: (bi, i, kcol)),
            pl.BlockSpec((1, QBLK, C_KVW), lambda bi, i: (bi, prev(bi, i), kcol + 1)),
            pl.BlockSpec((1, QBLK, C_KVW), lambda bi, i: (bi, i, kcol + 1)),
            pl.BlockSpec(memory_space=pltpu.SMEM),
            pl.BlockSpec((C_HEADS, 3, QBLK, QBLK), lambda bi, i: (0, 0, 0, 0)),
        ],
        out_specs=pl.BlockSpec((1, QBLK, C_Q), lambda bi, i: (bi, i, 0)),
        out_shape=jax.ShapeDtypeStruct((b, s, C_Q), BF16),
        compiler_params=_cparams("parallel", "parallel"),
        name="swa_attention",
    )(zo, zo, zo, zo, zo, sinks.astype(F32), tb)


def _router_kernel(h_ref, g_ref, wr_ref, xn_ref, idx_ref, wt_ref, *, n_experts):
    x = h_ref[...]
    xn = x * lax.rsqrt(jnp.mean(x * x, axis=-1, keepdims=True) + EPS) * g_ref[...]
    packed = _pack_bf16_pairs(xn)
    n_tiles = packed.shape[1] // LANES
    for s in range(n_tiles):
        xn_ref[pl.ds(s, x.shape[0], stride=n_tiles), :] = packed[:, s * LANES:(s + 1) * LANES]
    logits = jnp.dot(xn, wr_ref[...], preferred_element_type=F32, precision=lax.Precision.HIGHEST)
    lane = lax.broadcasted_iota(jnp.int32, logits.shape, 1)
    lg = jnp.where(lane < n_experts, logits, NEG)
    m1 = jnp.max(lg, axis=-1, keepdims=True)
    i1 = jnp.min(jnp.where(lg == m1, lane, LANES), axis=-1, keepdims=True)
    lg2 = jnp.where(lane == i1, NEG, lg)
    m2 = jnp.max(lg2, axis=-1, keepdims=True)
    i2 = jnp.min(jnp.where(lg2 == m2, lane, LANES), axis=-1, keepdims=True)
    e2 = jnp.exp(m2 - m1)
    idx_ref[...] = jnp.where(lane == 0, i1, jnp.where(lane == 1, i2, 0))
    wt_ref[...] = jnp.where(lane == 0, 1.0 / (1.0 + e2), jnp.where(lane == 1, e2 / (1.0 + e2), 0.0))


def router(h, g, w_router):
    t, d = h.shape
    e = w_router.shape[1]
    tm = _pick(t, 512)
    wr = jnp.zeros((d, LANES), F32).at[:, :e].set(w_router.astype(F32))
    row = lambda i: (i, 0)
    return pl.pallas_call(
        functools.partial(_router_kernel, n_experts=e),
        grid=(t // tm,),
        in_specs=[pl.BlockSpec((tm, d), row), pl.BlockSpec((1, d), lambda i: (0, 0)),
                  pl.BlockSpec((d, LANES), lambda i: (0, 0))],
        out_specs=[pl.BlockSpec((tm * (d // 2 // LANES), LANES), row), pl.BlockSpec((tm, LANES), row),
                   pl.BlockSpec((tm, LANES), row)],
        out_shape=[jax.ShapeDtypeStruct((t * (d // 2 // LANES), LANES), jnp.uint32), jax.ShapeDtypeStruct((t, LANES), jnp.int32),
                   jax.ShapeDtypeStruct((t, LANES), F32)],
        compiler_params=_cparams("parallel"),
        name="moe_router",
    )(h, g.reshape(1, d).astype(F32), wr)


def _row_copy(src_hbm, dst_ref, sem, src_row, dst_row):
    return pltpu.make_async_copy(src_hbm.at[pl.ds(src_row, 1), :], dst_ref.at[pl.ds(dst_row, 1), :], sem)


def _token_copy(src_hbm, dst_ref, sem, src_tok, dst_tok, n_tiles):
    return pltpu.make_async_copy(src_hbm.at[pl.ds(pl.multiple_of(src_tok * n_tiles, n_tiles), n_tiles), :],
                                 dst_ref.at[pl.ds(pl.multiple_of(dst_tok * n_tiles, n_tiles), n_tiles), :], sem)


def _gather_kernel(tok_ref, x_hbm, o_ref, buf_ref, sem):
    rows = o_ref.shape[0]
    n_tiles = buf_ref.shape[0] // rows

    def start(r2, c):
        for par in range(2):
            r = 2 * r2 + par
            _token_copy(x_hbm, buf_ref, sem, tok_ref[0, 0, r], r, n_tiles).start(priority=par)
        return c

    def wait(r, c):
        _token_copy(x_hbm, buf_ref, sem, 0, r, n_tiles).wait()
        return c

    lax.fori_loop(0, rows // 2, start, 0)
    lax.fori_loop(0, rows, wait, 0)
    packed = jnp.concatenate([buf_ref[pl.ds(s, rows, stride=n_tiles), :] for s in range(n_tiles)], axis=1)
    o_ref[...] = _unpack_bf16_pairs(packed).astype(o_ref.dtype)


def gather_rows(x, row_tok, tm, d, out_dtype):
    n_rows = row_tok.shape[0]
    n_tiles = d // 2 // LANES
    return pl.pallas_call(
        _gather_kernel,
        grid=(n_rows // tm,),
        in_specs=[pl.BlockSpec((1, 1, tm), lambda i: (i, 0, 0), memory_space=pltpu.SMEM),
                  pl.BlockSpec(memory_space=pl.ANY)],
        out_specs=pl.BlockSpec((tm, d), lambda i: (i, 0)),
        out_shape=jax.ShapeDtypeStruct((n_rows, d), out_dtype),
        scratch_shapes=[pltpu.VMEM((tm * n_tiles, LANES), x.dtype), pltpu.SemaphoreType.DMA(())],
        compiler_params=_cparams("arbitrary"),
        name="moe_gather",
    )(row_tok.reshape(n_rows // tm, 1, tm), x)


def _combine_kernel(p0_ref, p1_ref, y_hbm, wt_ref, h_ref, g_ref, o_ref, on_ref, a_ref, b_ref, sem, *, tile_words):
    rows = o_ref.shape[0]

    def start(r, c):
        _row_copy(y_hbm, a_ref, sem.at[0], p0_ref[0, 0, r], r).start(priority=0)
        _row_copy(y_hbm, b_ref, sem.at[1], p1_ref[0, 0, r], r).start(priority=1)
        return c

    def wait(r, c):
        _row_copy(y_hbm, a_ref, sem.at[0], 0, r).wait()
        _row_copy(y_hbm, b_ref, sem.at[1], 0, r).wait()
        return c

    lax.fori_loop(0, rows, start, 0)
    lax.fori_loop(0, rows, wait, 0)
    wt = wt_ref[...]

    def unpack(ref):
        return jnp.concatenate([_unpack_bf16_pairs(ref[:, c:c + tile_words])
                                for c in range(0, ref.shape[1], tile_words)], axis=1)

    out = h_ref[...] + wt[:, 0:1] * unpack(a_ref) + wt[:, 1:2] * unpack(b_ref)
    o_ref[...] = out
    normed = out * lax.rsqrt(jnp.mean(out * out, axis=-1, keepdims=True) + EPS)
    on_ref[...] = (normed * g_ref[...]).astype(on_ref.dtype)


def combine_rows(y_rows, pos0, pos1, wt, h, g_next, tm, tile_words):
    t, d = h.shape
    idx_spec = pl.BlockSpec((1, 1, tm), lambda i: (i, 0, 0), memory_space=pltpu.SMEM)
    return pl.pallas_call(
        functools.partial(_combine_kernel, tile_words=tile_words),
        grid=(t // tm,),
        in_specs=[idx_spec, idx_spec, pl.BlockSpec(memory_space=pl.ANY),
                  pl.BlockSpec((tm, LANES), lambda i: (i, 0)), pl.BlockSpec((tm, d), lambda i: (i, 0)),
                  pl.BlockSpec((1, d), lambda i: (0, 0))],
        out_specs=[pl.BlockSpec((tm, d), lambda i: (i, 0)), pl.BlockSpec((tm, d), lambda i: (i, 0))],
        out_shape=[jax.ShapeDtypeStruct((t, d), F32), jax.ShapeDtypeStruct((t, d), BF16)],
        scratch_shapes=[pltpu.VMEM((tm, d // 2), jnp.uint32), pltpu.VMEM((tm, d // 2), jnp.uint32),
                        pltpu.SemaphoreType.DMA((2,))],
        compiler_params=_cparams("arbitrary"),
        name="moe_combine",
    )(pos0.reshape(t // tm, 1, tm), pos1.reshape(t // tm, 1, tm), y_rows, wt, h, g_next.reshape(1, d).astype(F32))


def moe_layer(h, g, g_next, w_router, wg, wu, wd, layer, tm_pref=512):
    t, d = h.shape
    n_exp = w_router.shape[1]
    tm = _pick(t, tm_pref)
    xn, idx, wt = router(h, g, w_router)
    e_flat = idx[:, :TOP_K].reshape(-1)
    onehot = (e_flat[:, None] == jnp.arange(n_exp)[None, :]).astype(jnp.int32)
    csum = jnp.cumsum(onehot, axis=0)
    rank = jnp.sum((csum - onehot) * onehot, axis=1)
    counts = csum[-1]
    padded = (counts + tm - 1) // tm * tm
    pad_end = jnp.cumsum(padded)
    dest = (pad_end - padded)[e_flat] + rank
    n_rows = t * TOP_K + n_exp * tm
    row_tok = jnp.zeros((n_rows,), jnp.int32).at[dest].set(jnp.arange(t * TOP_K, dtype=jnp.int32) // TOP_K)
    n_blk = n_rows // tm
    blk = jnp.arange(n_blk)
    n_used = (pad_end[-1:] // tm).astype(jnp.int32)
    blk_expert = jnp.minimum(jnp.searchsorted(pad_end, blk * tm, side='right'), n_exp - 1).astype(jnp.int32)
    blk_expert = jnp.where(blk < n_used[0], blk_expert, blk_expert[n_used[0] - 1])
    first = ((blk == 0) | (blk_expert != jnp.roll(blk_expert, 1))).astype(jnp.int32)
    groups = (layer * n_exp + blk_expert, first, n_used)

    xs = gather_rows(xn, row_tok, tm, d, BF16)
    hid = grouped_glu(xs, wg, wu, groups, tm)
    tn_down = _pick(d, 512)
    y_rows = grouped_matmul([hid], wd, groups, F32, tm, tn_down, pack_out=True)
    pos = dest.reshape(t, TOP_K).astype(jnp.int32)
    return combine_rows(y_rows, pos[:, 0], pos[:, 1], wt, h, g_next, tm, tn_down // 2)


def _even_in_proj_weight(w_in):
    d = w_in.shape[0]
    g0 = A_Q + 6 * A_KVW
    gates = w_in[:, g0:g0 + 3 * A_HEADS].reshape(d, 3, A_KV, A_GRP)
    blocks = [w_in[:, :g0], w_in[:, g0 + 3 * A_HEADS:]]
    for k in range(A_KV):
        gk = gates[:, :, k, :].reshape(d, 3 * A_GRP)
        blocks.append(jnp.pad(gk, ((0, 0), (0, LANES - 3 * A_GRP))))
    w = jnp.concatenate(blocks, axis=1)
    return jnp.pad(w, ((0, 0), (0, ZE_WIDTH - w.shape[1])))


def kernel(x, p, rel_bias, norm_mix, norm_ffn, norm_ple, norm_final, w_in_e, cmp_pos, cmp_w1, cmp_w2, conv_w,
           w_out_e, w_gate_d, w_up_d, w_down_d, w_qkv_o, sinks, w_out_o, w_router, w_gate_m, w_up_m, w_down_m,
           w_ple, w_ple_gate):
    b, s, d = x.shape
    t = b * s
    depth = norm_mix.shape[0]
    tm = _pick(t, 1024)
    tm_down = _pick(t, 512)

    tw, tc = nsa_bias_tiles(rel_bias, s)
    tb = bias_table(swa_bucket_idx(), rel_bias, A_HEADS, C_HEADS)

    w_in_relaid = jnp.stack([_even_in_proj_weight(w_in_e[j]) for j in range(w_in_e.shape[0])])
    w_gate_m = w_gate_m.reshape((-1,) + w_gate_m.shape[2:])
    w_up_m = w_up_m.reshape((-1,) + w_up_m.shape[2:])
    w_down_m = w_down_m.reshape((-1,) + w_down_m.shape[2:])
    p = p.reshape(depth, t, -1)

    h = x.reshape(t, d).astype(F32)
    for i in range(depth):
        j = i // 2
        layer = _dense_groups(t // tm, j)
        half_blocks = _dense_groups(t // tm_down, j)
        hn = rmsnorm(h, norm_mix[i], BF16)
        if i % 2 == 0:
            ze = grouped_matmul([hn], w_in_relaid, layer, BF16, tm, 2048).reshape(b, s, ZE_WIDTH)
            kvcm = compress(ze, cmp_pos[j], cmp_w1[j], cmp_w2[j])
            att = nsa_attention(ze, kvcm, tw, tc).reshape(t, A_Q)
            cnv = short_conv(ze, conv_w[j]).reshape(t, CONV_DIM)
            h, hn = grouped_matmul([att, cnv], w_out_e, half_blocks, F32, tm_down, d, res=h, norm_gain=norm_ffn[i])
            hid = grouped_glu(hn, w_gate_d, w_up_d, half_blocks, tm_down)
            h = grouped_matmul([hid], w_down_d, half_blocks, F32, tm_down, 512, res=h)
        else:
            zo = grouped_matmul([hn], w_qkv_o, layer, BF16, tm, 1280).reshape(b, s, -1)
            att = swa_attention(zo, sinks[j], tb).reshape(t, C_Q)
            h = grouped_matmul([att], w_out_o, half_blocks, F32, tm_down, d, res=h)
            h, hn = moe_layer(h, norm_ffn[i], norm_ple[i], w_router[j], w_gate_m, w_up_m, w_down_m, j)
        if i % 2 == 0:
            hn = rmsnorm(h, norm_ple[i], BF16)
        h = ple(hn, p, w_ple_gate, w_ple, i, h)
    return rmsnorm(h, norm_final, x.dtype).reshape(b, s, d)
```

```python
import functools
import math

import numpy as np
import jax
import jax.numpy as jnp
from jax import lax
from jax.experimental import pallas as pl
from jax.experimental.pallas import tpu as pltpu

F32 = jnp.float32
BF16 = jnp.bfloat16

LANES = 128
VMEM_LIMIT_BYTES = 56 * 1024 * 1024
VMEM_LIMIT_STAGED_BYTES = 60 * 1024 * 1024

EPS = 1e-6
A_HEADS = 8
A_KV = 2
A_GRP = A_HEADS // A_KV
A_DH = 128
A_Q = A_HEADS * A_DH
A_KVW = A_KV * A_DH
CMP_LEN = 32
CMP_STRIDE = 16
SLC_BLK = 64
SLC_TOPK = 16
WIN_A = 512
CONV_DIM = 1024
CONV_K = 3
C_HEADS = 32
C_KV = 4
C_GRP = C_HEADS // C_KV
C_DH = 64
C_Q = C_HEADS * C_DH
C_KVW = C_KV * C_DH
WIN_C = 128
N_BUCKETS = 32
MAX_DIST = 1024
TOP_K = 2

QBLK = 128
FAR_REL = 8
NEG = -1e30
BIG = 1e30

ZE_Q = 0
ZE_KC, ZE_VC, ZE_KS, ZE_VS, ZE_KW, ZE_VW = 8, 10, 12, 14, 16, 18
ZE_GB, ZE_GC, ZE_HC = 20, 28, 36
ZE_GATE = 44
ZE_WIDTH = 48 * LANES


def _cparams(*sem, vmem_limit_bytes=VMEM_LIMIT_BYTES):
    return pltpu.CompilerParams(dimension_semantics=sem, vmem_limit_bytes=vmem_limit_bytes)


def _sigmoid(v):
    return 1.0 / (1.0 + jnp.exp(-v))


def _pick(n, pref):
    t = min(n, pref)
    while n % t:
        t -= LANES if t > LANES else 8
    return t


def _pack_bf16_pairs(x):
    half = x.shape[1] // 2
    lo = lax.bitcast_convert_type(x[:, :half].astype(BF16).astype(F32), jnp.uint32)
    hi = lax.bitcast_convert_type(x[:, half:].astype(BF16).astype(F32), jnp.uint32)
    return lax.shift_right_logical(lo, jnp.uint32(16)) | hi


def _unpack_bf16_pairs(w):
    lo = lax.bitcast_convert_type(lax.shift_left(w, jnp.uint32(16)), F32)
    hi = lax.bitcast_convert_type(w & jnp.uint32(0xFFFF0000), F32)
    return jnp.concatenate([lo, hi], axis=1)


def _rmsnorm_kernel(x_ref, g_ref, o_ref):
    x = x_ref[...]
    y = x * lax.rsqrt(jnp.mean(x * x, axis=-1, keepdims=True) + EPS)
    o_ref[...] = (y * g_ref[...]).astype(o_ref.dtype)


def rmsnorm(x, g, out_dtype):
    t, d = x.shape
    tm = _pick(t, 512)
    return pl.pallas_call(
        _rmsnorm_kernel,
        grid=(t // tm,),
        in_specs=[pl.BlockSpec((tm, d), lambda i: (i, 0)), pl.BlockSpec((1, d), lambda i: (0, 0))],
        out_specs=pl.BlockSpec((tm, d), lambda i: (i, 0)),
        out_shape=jax.ShapeDtypeStruct((t, d), out_dtype),
        compiler_params=_cparams("parallel"),
        name="rmsnorm",
    )(x, g.reshape(1, d).astype(F32))


def _dense_groups(n_blk, layer):
    grp = jnp.full((n_blk,), layer, jnp.int32)
    first = (jnp.arange(n_blk) == 0).astype(jnp.int32)
    return grp, first, jnp.full((1,), n_blk, jnp.int32)


def _next_group(groups):
    grp, first, n_used = groups
    n_blk = grp.shape[0]
    blk = jnp.arange(n_blk)
    starts = jnp.where((first == 1) & (blk < n_used[0]), blk, n_blk)
    after = jnp.concatenate([jnp.flip(lax.cummin(jnp.flip(starts)))[1:], jnp.full((1,), n_blk, starts.dtype)])
    return jnp.where(after < n_blk, grp[jnp.minimum(after, n_blk - 1)], -1).astype(jnp.int32)


def _stage_weights(grp_ref, first_ref, nxt_ref, w_hbms, stage_ref, wb_ref, sem, tn):
    j = pl.program_id(0)
    i = pl.program_id(1)

    def tile_copies(grp, col_tile):
        col = pl.multiple_of(col_tile * tn, LANES)
        return [pltpu.make_async_copy(w.at[grp, :, pl.ds(col, tn)], stage_ref.at[k], sem.at[k])
                for k, w in enumerate(w_hbms)]

    @pl.when(first_ref[i] == 1)
    def _():
        @pl.when((i == 0) & (j == 0))
        def _():
            for cp in tile_copies(grp_ref[0], 0):
                cp.start()

        for cp in tile_copies(grp_ref[i], j):
            cp.wait()
        for k in range(len(w_hbms)):
            wb_ref[k] = stage_ref[k].astype(BF16)
        nxt = nxt_ref[i]

        @pl.when(nxt >= 0)
        def _():
            for cp in tile_copies(nxt, j):
                cp.start()

        @pl.when((nxt < 0) & (j + 1 < pl.num_programs(0)))
        def _():
            for cp in tile_copies(grp_ref[0], j + 1):
                cp.start()


def _gmm_kernel(grp_ref, first_ref, nu_ref, nxt_ref, *refs, k_sizes, has_res, has_norm, pack_out):
    n_x = len(k_sizes)
    w_hbm = refs[n_x]
    stage_ref, wb_ref, sem = refs[-3:]
    outs = refs[n_x + 1 + has_res + has_norm:-3]
    o_ref = outs[0]
    i = pl.program_id(1)
    _stage_weights(grp_ref, first_ref, nxt_ref, [w_hbm], stage_ref, wb_ref, sem, wb_ref.shape[2])

    @pl.when(i < nu_ref[0])
    def _():
        acc = None
        off = 0
        for x_ref, k in zip(refs[:n_x], k_sizes):
            part = jnp.dot(x_ref[...], wb_ref[0, off:off + k, :], preferred_element_type=F32)
            acc = part if acc is None else acc + part
            off += k
        if has_res:
            acc = refs[n_x + 1][...] + acc
        o_ref[...] = _pack_bf16_pairs(acc) if pack_out else acc.astype(o_ref.dtype)
        if has_norm:
            normed = acc * lax.rsqrt(jnp.mean(acc * acc, axis=-1, keepdims=True) + EPS)
            outs[1][...] = (normed * refs[n_x + 1 + has_res][...]).astype(outs[1].dtype)

    @pl.when(i >= nu_ref[0])
    def _():
        for o in outs:
            o[...] = jnp.zeros(o.shape, o.dtype)


def grouped_matmul(xs, w, groups, out_dtype, tm, tn_pref, res=None, norm_gain=None, pack_out=False):
    t = xs[0].shape[0]
    kw, n = w.shape[1], w.shape[2]
    k_sizes = tuple(x.shape[1] for x in xs)
    tn = _pick(n, tn_pref)
    assert norm_gain is None or tn == n
    idx = lambda j, i, g, f, nu, nx: (i, j)
    in_specs = [pl.BlockSpec((tm, k), lambda j, i, g, f, nu, nx: (i, 0)) for k in k_sizes]
    in_specs.append(pl.BlockSpec(memory_space=pl.ANY))
    args = list(xs) + [w]
    if res is not None:
        in_specs.append(pl.BlockSpec((tm, tn), idx))
        args.append(res)
    if norm_gain is not None:
        in_specs.append(pl.BlockSpec((1, tn), lambda j, i, g, f, nu, nx: (0, 0)))
        args.append(norm_gain.reshape(1, n).astype(F32))
    out_specs = [pl.BlockSpec((tm, tn // 2 if pack_out else tn), idx)]
    out_shape = [jax.ShapeDtypeStruct((t, n // 2), jnp.uint32) if pack_out else jax.ShapeDtypeStruct((t, n), out_dtype)]
    if norm_gain is not None:
        out_specs.append(pl.BlockSpec((tm, tn), idx))
        out_shape.append(jax.ShapeDtypeStruct((t, n), BF16))
    grid_spec = pltpu.PrefetchScalarGridSpec(
        num_scalar_prefetch=4,
        grid=(n // tn, t // tm),
        in_specs=in_specs,
        out_specs=out_specs,
        scratch_shapes=[pltpu.VMEM((1, kw, tn), F32), pltpu.VMEM((1, kw, tn), BF16), pltpu.SemaphoreType.DMA((1,))],
    )
    out = pl.pallas_call(
        functools.partial(_gmm_kernel, k_sizes=k_sizes, has_res=res is not None, has_norm=norm_gain is not None,
                          pack_out=pack_out),
        grid_spec=grid_spec,
        out_shape=out_shape,
        compiler_params=_cparams("arbitrary", "arbitrary", vmem_limit_bytes=VMEM_LIMIT_STAGED_BYTES),
        name="matmul",
    )(*groups, _next_group(groups), *args)
    return out if norm_gain is not None else out[0]


def _glu_kernel(grp_ref, first_ref, nu_ref, nxt_ref, x_ref, wg_hbm, wu_hbm, o_ref, stage_ref, wb_ref, sem):
    i = pl.program_id(1)
    _stage_weights(grp_ref, first_ref, nxt_ref, [wg_hbm, wu_hbm], stage_ref, wb_ref, sem, o_ref.shape[1])

    @pl.when(i < nu_ref[0])
    def _():
        x = x_ref[...]
        g = jnp.dot(x, wb_ref[0], preferred_element_type=F32)
        u = jnp.dot(x, wb_ref[1], preferred_element_type=F32)
        o_ref[...] = (g * _sigmoid(g) * u).astype(o_ref.dtype)

    @pl.when(i >= nu_ref[0])
    def _():
        o_ref[...] = jnp.zeros(o_ref.shape, o_ref.dtype)


def grouped_glu(x, wg, wu, groups, tm, tf_pref=1408):
    t, d = x.shape
    f = wg.shape[2]
    tf = _pick(f, tf_pref)
    grid_spec = pltpu.PrefetchScalarGridSpec(
        num_scalar_prefetch=4,
        grid=(f // tf, t // tm),
        in_specs=[pl.BlockSpec((tm, d), lambda j, i, g, fl, nu, nx: (i, 0)),
                  pl.BlockSpec(memory_space=pl.ANY), pl.BlockSpec(memory_space=pl.ANY)],
        out_specs=pl.BlockSpec((tm, tf), lambda j, i, g, fl, nu, nx: (i, j)),
        scratch_shapes=[pltpu.VMEM((2, d, tf), F32), pltpu.VMEM((2, d, tf), BF16), pltpu.SemaphoreType.DMA((2,))],
    )
    return pl.pallas_call(
        _glu_kernel,
        grid_spec=grid_spec,
        out_shape=jax.ShapeDtypeStruct((t, f), BF16),
        compiler_params=_cparams("arbitrary", "arbitrary", vmem_limit_bytes=VMEM_LIMIT_STAGED_BYTES),
        name="glu",
    )(*groups, _next_group(groups), x, wg, wu)


def _ple_kernel(grp_ref, first_ref, nu_ref, nxt_ref, hn_ref, p_ref, wg_hbm, wp_ref, h_ref, gn_ref, o_ref, on_ref,
                stage_ref, wb_ref, sem, wpb_ref):
    _stage_weights(grp_ref, first_ref, nxt_ref, [wg_hbm], stage_ref, wb_ref, sem, wb_ref.shape[2])

    @pl.when(pl.program_id(1) == 0)
    def _():
        wpb_ref[...] = wp_ref[...].astype(BF16)

    gate = _sigmoid(jnp.dot(hn_ref[...], wb_ref[0], preferred_element_type=F32))
    pe = jnp.dot(p_ref[...].astype(BF16), wpb_ref[...], preferred_element_type=F32)
    out = h_ref[...] + gate * pe
    o_ref[...] = out
    normed = out * lax.rsqrt(jnp.mean(out * out, axis=-1, keepdims=True) + EPS)
    on_ref[...] = (normed * gn_ref[...]).astype(on_ref.dtype)


def ple(hn, p, wg, wp, layer, h, g_next, next_dtype, tm_pref=256):
    t, d = h.shape
    pd = p.shape[2]
    tm = _pick(t, tm_pref)
    row = lambda j, i, g, f, nu, nx: (i, 0)
    grid_spec = pltpu.PrefetchScalarGridSpec(
        num_scalar_prefetch=4,
        grid=(1, t // tm),
        in_specs=[
            pl.BlockSpec((tm, d), row),
            pl.BlockSpec((None, tm, pd), lambda j, i, g, f, nu, nx: (layer, i, 0)),
            pl.BlockSpec(memory_space=pl.ANY),
            pl.BlockSpec((None, pd, d), lambda j, i, g, f, nu, nx: (layer, 0, 0)),
            pl.BlockSpec((tm, d), row),
            pl.BlockSpec((1, d), lambda j, i, g, f, nu, nx: (0, 0)),
        ],
        out_specs=[pl.BlockSpec((tm, d), row), pl.BlockSpec((tm, d), row)],
        scratch_shapes=[pltpu.VMEM((1, d, d), F32), pltpu.VMEM((1, d, d), BF16), pltpu.SemaphoreType.DMA((1,)),
                        pltpu.VMEM((pd, d), BF16)],
    )
    groups = _dense_groups(t // tm, layer)
    return pl.pallas_call(
        _ple_kernel,
        grid_spec=grid_spec,
        out_shape=[jax.ShapeDtypeStruct((t, d), F32), jax.ShapeDtypeStruct((t, d), next_dtype)],
        compiler_params=_cparams("arbitrary", "arbitrary", vmem_limit_bytes=VMEM_LIMIT_STAGED_BYTES),
        name="ple",
    )(*groups, _next_group(groups), hn, p, wg, wp, h, g_next.reshape(1, d).astype(F32))


def t5_bucket(dist):
    n = jnp.maximum(jnp.asarray(dist, jnp.int32), 0)
    exact = N_BUCKETS // 2
    nf = jnp.maximum(n, 1).astype(F32)
    large = exact + (jnp.log(nf / exact) / math.log(MAX_DIST / exact) * (N_BUCKETS - exact)).astype(jnp.int32)
    return jnp.where(n < exact, n, jnp.minimum(large, N_BUCKETS - 1))


def _bias_table_kernel(idx_ref, tab_ref, o_ref, *, head0):
    h = head0 + pl.program_id(0)
    idx = idx_ref[0]
    val = jnp.full(idx.shape, tab_ref[0, h], F32)
    for b in range(1, N_BUCKETS):
        val = jnp.where(idx == b, tab_ref[b, h], val)
    o_ref[0, 0] = jnp.where(idx < 0, NEG, val)


def bias_table(bucket_idx, rel_bias, head0, n_heads):
    n, r, c = bucket_idx.shape
    return pl.pallas_call(
        functools.partial(_bias_table_kernel, head0=head0),
        grid=(n_heads, n),
        in_specs=[
            pl.BlockSpec((1, r, c), lambda h, i: (i, 0, 0)),
            pl.BlockSpec(memory_space=pltpu.SMEM),
        ],
        out_specs=pl.BlockSpec((1, 1, r, c), lambda h, i: (h, i, 0, 0)),
        out_shape=jax.ShapeDtypeStruct((n_heads, n, r, c), F32),
        compiler_params=_cparams("parallel", "parallel"),
        name="bias_table",
    )(bucket_idx, rel_bias.astype(F32))


def tile_bucket_idx(n_rel):
    a = np.arange(QBLK)[None, :, None]
    c = np.arange(QBLK)[None, None, :]
    r = np.arange(n_rel)[:, None, None]
    return t5_bucket(QBLK * r + a - c)


def cmp_bucket_idx(s, n_cmp_pad):
    t = np.arange(s).reshape(s // QBLK, QBLK, 1)
    cend = (np.arange(n_cmp_pad) * CMP_STRIDE + CMP_LEN - 1)[None, None, :]
    return t5_bucket(t - cend)


def _compress_kernel(t_ref, pos_ref, w1_ref, w2_ref, o_ref, tf_ref, *, n_pad):
    s = t_ref.shape[1]
    half = CMP_LEN // 2
    tf_ref[0:s, :] = t_ref[0].astype(F32)
    tf_ref[s:s + half, :] = jnp.zeros((half, A_DH), F32)
    hid = jnp.zeros((n_pad, w1_ref.shape[2]), F32)
    for l in range(CMP_LEN):
        rows = tf_ref[pl.ds(l, n_pad, stride=CMP_STRIDE), :] + pos_ref[0, l:l + 1, :]
        hid = hid + jnp.dot(rows.astype(BF16), w1_ref[0, l * A_DH:(l + 1) * A_DH, :], preferred_element_type=F32)
    act = jax.nn.gelu(hid)
    out = jnp.dot(act.astype(BF16), w2_ref[0], preferred_element_type=F32)
    valid = lax.broadcasted_iota(jnp.int32, out.shape, 0) < n_pad - 1
    o_ref[0, 0, 0] = jnp.where(valid, out, 0.0).astype(o_ref.dtype)


def compress(ze, cmp_pos, cmp_w1, cmp_w2):
    b, s, _ = ze.shape
    n_pad = s // CMP_STRIDE
    hid = cmp_w1.shape[2]
    return pl.pallas_call(
        functools.partial(_compress_kernel, n_pad=n_pad),
        grid=(2, b, A_KV),
        in_specs=[
            pl.BlockSpec((1, s, A_DH), lambda w, bi, k: (bi, 0, ZE_KC + 2 * w + k)),
            pl.BlockSpec((1, CMP_LEN, A_DH), lambda w, bi, k: (w, 0, 0)),
            pl.BlockSpec((1, CMP_LEN * A_DH, hid), lambda w, bi, k: (w, 0, 0)),
            pl.BlockSpec((1, hid, A_DH), lambda w, bi, k: (w, 0, 0)),
        ],
        out_specs=pl.BlockSpec((1, 1, 1, n_pad, A_DH), lambda w, bi, k: (w, bi, k, 0, 0)),
        out_shape=jax.ShapeDtypeStruct((2, b, A_KV, n_pad, A_DH), BF16),
        scratch_shapes=[pltpu.VMEM((s + CMP_LEN // 2, A_DH), F32)],
        compiler_params=_cparams("parallel", "parallel", "parallel"),
        name="nsa_compress",
    )(ze, cmp_pos.astype(F32), cmp_w1.astype(BF16), cmp_w2.astype(BF16))


def _nsa_kernel(q_ref, kcm_ref, vcm_ref, ks_ref, vs_ref, kw_ref, vw_ref, g_ref, tw_ref, tc_ref, o_ref,
                qs_ref, vct_ref, vst_ref, vwt_ref, sel_ref, m_ref, l_ref, acc_ref):
    i = pl.program_id(2)
    tq = QBLK
    hq = A_GRP * tq
    s_len = ks_ref.shape[1]
    n_cmp = kcm_ref.shape[3]
    n_slc = s_len // SLC_BLK
    nt = (((1,), (1,)), ((), ()))
    heads = [slice(g * tq, (g + 1) * tq) for g in range(A_GRP)]

    def per_head(x):
        return jnp.concatenate([x] * A_GRP, axis=1)

    @pl.when(i == 0)
    def _():
        vct_ref[...] = vcm_ref[0, 0, 0].astype(F32).T.astype(BF16)

        def transpose_tile(t, c):
            off = pl.multiple_of(t * tq, tq)
            vst_ref[:, pl.ds(off, tq)] = vs_ref[0, pl.ds(off, tq), :].astype(F32).T.astype(BF16)
            vwt_ref[:, pl.ds(off, tq)] = vw_ref[0, pl.ds(off, tq), :].astype(F32).T.astype(BF16)
            return c

        lax.fori_loop(0, s_len // tq, transpose_tile, 0)

    q = q_ref[0]
    for g in range(A_GRP):
        qs_ref[heads[g], :] = (q[:, g * A_DH:(g + 1) * A_DH].astype(F32) * (A_DH ** -0.5)).astype(BF16)
    t_row = i * tq + lax.broadcasted_iota(jnp.int32, (1, tq), 1)

    n_col = lax.broadcasted_iota(jnp.int32, (n_cmp, hq), 0)
    t_all = i * tq + (lax.broadcasted_iota(jnp.int32, (n_cmp, hq), 1) & (tq - 1))
    mask_c = (t_all >= n_col * CMP_STRIDE + (CMP_LEN - 1)) & (n_col < n_cmp - 1)
    s_c = lax.dot_general(kcm_ref[0, 0, 0], qs_ref[...], nt, preferred_element_type=F32) + tc_ref[0, 0]
    s_c = jnp.where(mask_c, s_c, NEG)
    e_c = jnp.where(mask_c, jnp.exp(s_c - jnp.max(s_c, axis=0, keepdims=True)), 0.0)
    r_c = 1.0 / jnp.maximum(jnp.sum(e_c, axis=0, keepdims=True), 1e-30)
    o_c = jnp.dot(vct_ref[...], e_c.astype(BF16), preferred_element_type=F32) * r_c
    p_c = e_c * r_c
    p_sum = (p_c[:, heads[0]] + p_c[:, heads[1]]) + (p_c[:, heads[2]] + p_c[:, heads[3]])
    ov_j = lax.broadcasted_iota(jnp.int32, (LANES, n_cmp), 0) * SLC_BLK
    ov_n = lax.broadcasted_iota(jnp.int32, (LANES, n_cmp), 1) * CMP_STRIDE
    overlap_t = ((ov_n < ov_j + SLC_BLK) & (ov_n + CMP_LEN > ov_j)).astype(F32)
    imp_t = jnp.dot(overlap_t, p_sum, preferred_element_type=F32, precision=lax.Precision.HIGHEST)

    jb = lax.broadcasted_iota(jnp.int32, (LANES, tq), 0)
    cur = t_row // SLC_BLK
    imp_t = jnp.where(jb * SLC_BLK > t_row, -BIG, imp_t)
    imp_t = jnp.where((jb == 0) | (jb == cur) | (jb == cur - 1), BIG, imp_t)
    sub = 8
    slabs = [imp_t[r * sub:(r + 1) * sub] for r in range(n_slc // sub)]
    jb_slab = lax.broadcasted_iota(jnp.int32, (sub, tq), 0)
    ranks = [jnp.zeros((sub, tq), F32) for _ in slabs]
    for j2 in range(n_slc):
        other = imp_t[j2:j2 + 1, :]
        for r, v in enumerate(slabs):
            if r * sub > j2:
                ahead = other >= v
            elif r * sub + sub - 1 <= j2:
                ahead = other > v
            else:
                ahead = (other > v) | ((other == v) & (jb_slab + r * sub > j2))
            ranks[r] = ranks[r] + jnp.where(ahead, 1.0, 0.0)
    sel_ref[...] = jnp.concatenate([jnp.where(rk < float(min(SLC_TOPK, n_slc)), 1.0, 0.0) for rk in ranks], axis=0)

    n_w = WIN_A // tq + 1
    jw0 = jnp.maximum(i - WIN_A // tq, 0)
    w_start = pl.multiple_of(jw0 * tq, tq)
    dist_w = t_row - (w_start + lax.broadcasted_iota(jnp.int32, (n_w * tq, tq), 0))
    madd_w = jnp.where((dist_w >= 0) & (dist_w < WIN_A), 0.0, NEG)
    s_w = lax.dot_general(kw_ref[0, pl.ds(w_start, n_w * tq), :], qs_ref[...], nt, preferred_element_type=F32)
    s_w = s_w + jnp.concatenate([tw_ref[0, jnp.clip(i - jw0 - cb, 0, FAR_REL)] for cb in range(n_w)], axis=0)
    s_w = s_w + per_head(madd_w)
    e_w = jnp.exp(s_w - jnp.max(s_w, axis=0, keepdims=True))
    r_w = 1.0 / jnp.maximum(jnp.sum(e_w, axis=0, keepdims=True), 1e-30)
    o_w = jnp.dot(vwt_ref[:, pl.ds(w_start, n_w * tq)], e_w.astype(BF16), preferred_element_type=F32) * r_w

    ck = 4 * tq
    blocks_per_chunk = ck // SLC_BLK
    m_ref[...] = jnp.full(m_ref.shape, NEG, F32)
    l_ref[...] = jnp.zeros(l_ref.shape, F32)
    acc_ref[...] = jnp.zeros(acc_ref.shape, F32)
    key_iota = lax.broadcasted_iota(jnp.int32, (ck, tq), 0)

    def chunk(c, carry):
        start = pl.multiple_of(c * ck, ck)
        sel_rows = sel_ref[pl.ds(pl.multiple_of(c * blocks_per_chunk, blocks_per_chunk), blocks_per_chunk), :]
        chosen = jnp.concatenate([jnp.broadcast_to(sel_rows[b:b + 1, :], (SLC_BLK, tq))
                                  for b in range(blocks_per_chunk)], axis=0)
        madd = jnp.where(jnp.where(start + key_iota <= t_row, chosen, 0.0) > 0.5, 0.0, NEG)
        s_s = lax.dot_general(ks_ref[0, pl.ds(start, ck), :], qs_ref[...], nt, preferred_element_type=F32)
        s_s = s_s + jnp.concatenate([tw_ref[0, jnp.clip(i - (ck // tq) * c - cb, 0, FAR_REL)]
                                     for cb in range(ck // tq)], axis=0)
        s_s = s_s + per_head(madd)
        m_old = m_ref[...]
        m_new = jnp.maximum(m_old, jnp.max(s_s, axis=0, keepdims=True))
        p = jnp.exp(s_s - m_new)
        alpha = jnp.exp(m_old - m_new)
        l_ref[...] = alpha * l_ref[...] + jnp.sum(p, axis=0, keepdims=True)
        acc_ref[...] = alpha * acc_ref[...] + jnp.dot(vst_ref[:, pl.ds(start, ck)], p.astype(BF16),
                                                      preferred_element_type=F32)
        m_ref[...] = m_new
        return carry

    lax.fori_loop(0, (i * tq) // ck + 1, chunk, 0)
    o_s = acc_ref[...] / jnp.maximum(l_ref[...], 1e-30)

    gate_t = _sigmoid(g_ref[0].astype(F32)).T
    outs = []
    for g in range(A_GRP):
        o_t = (gate_t[g:g + 1, :] * o_c[:, heads[g]]
               + gate_t[A_GRP + g:A_GRP + g + 1, :] * o_s[:, heads[g]]
               + gate_t[2 * A_GRP + g:2 * A_GRP + g + 1, :] * o_w[:, heads[g]])
        outs.append(o_t.T)
    o_ref[0] = jnp.concatenate(outs, axis=-1).astype(o_ref.dtype)


def nsa_attention(ze, kvcm, tw, tc):
    b, s, _ = ze.shape
    n_cmp = kvcm.shape[3]
    hq = A_GRP * QBLK

    def kv_spec(col):
        return pl.BlockSpec((1, s, A_DH), lambda bi, k, i: (bi, 0, col + k))

    return pl.pallas_call(
        _nsa_kernel,
        grid=(b, A_KV, s // QBLK),
        in_specs=[
            pl.BlockSpec((1, QBLK, A_GRP * A_DH), lambda bi, k, i: (bi, i, k)),
            pl.BlockSpec((1, 1, 1, n_cmp, A_DH), lambda bi, k, i: (0, bi, k, 0, 0)),
            pl.BlockSpec((1, 1, 1, n_cmp, A_DH), lambda bi, k, i: (1, bi, k, 0, 0)),
            kv_spec(ZE_KS), kv_spec(ZE_VS), kv_spec(ZE_KW), kv_spec(ZE_VW),
            pl.BlockSpec((1, QBLK, LANES), lambda bi, k, i: (bi, i, ZE_GATE + k)),
            pl.BlockSpec((1, FAR_REL + 1, QBLK, hq), lambda bi, k, i: (k, 0, 0, 0)),
            pl.BlockSpec((1, 1, n_cmp, hq), lambda bi, k, i: (k, i, 0, 0)),
        ],
        out_specs=pl.BlockSpec((1, QBLK, A_GRP * A_DH), lambda bi, k, i: (bi, i, k)),
        out_shape=jax.ShapeDtypeStruct((b, s, A_Q), BF16),
        scratch_shapes=[pltpu.VMEM((hq, A_DH), BF16), pltpu.VMEM((A_DH, n_cmp), BF16),
                        pltpu.VMEM((A_DH, s), BF16), pltpu.VMEM((A_DH, s), BF16),
                        pltpu.VMEM((s // SLC_BLK, QBLK), F32),
                        pltpu.VMEM((1, hq), F32), pltpu.VMEM((1, hq), F32), pltpu.VMEM((A_DH, hq), F32)],
        compiler_params=_cparams("parallel", "parallel", "arbitrary"),
        name="nsa_attention",
    )(ze, kvcm, kvcm, ze, ze, ze, ze, ze, tw, tc)


def nsa_bias_tiles(rel_bias, s):
    n_cmp = s // CMP_STRIDE
    tw = bias_table(jnp.swapaxes(tile_bucket_idx(FAR_REL + 1), 1, 2), rel_bias, 0, A_HEADS)
    tw = tw.reshape(A_KV, A_GRP, FAR_REL + 1, QBLK, QBLK).transpose(0, 2, 3, 1, 4)
    tc = bias_table(jnp.swapaxes(cmp_bucket_idx(s, n_cmp), 1, 2), rel_bias, 0, A_HEADS)
    tc = tc.reshape(A_KV, A_GRP, s // QBLK, n_cmp, QBLK).transpose(0, 2, 3, 1, 4)
    return (tw.reshape(A_KV, FAR_REL + 1, QBLK, A_GRP * QBLK), tc.reshape(A_KV, s // QBLK, n_cmp, A_GRP * QBLK))


def _conv_kernel(gb_ref, gc_ref, hc_ref, gcp_ref, hcp_ref, w_ref, o_ref):
    i = pl.program_id(1)
    u = gc_ref[0].astype(F32) * hc_ref[0].astype(F32)
    prev = gcp_ref[0].astype(F32) * hcp_ref[0].astype(F32)
    prev = jnp.where(i > 0, prev, 0.0)
    n_prev = prev.shape[0]
    p1 = prev[n_prev - 1:n_prev, :]
    p2 = prev[n_prev - 2:n_prev - 1, :]
    row = lax.broadcasted_iota(jnp.int32, u.shape, 0)
    u1 = jnp.where(row == 0, p1, pltpu.roll(u, 1, 0))
    u2 = jnp.where(row == 0, p2, jnp.where(row == 1, p1, pltpu.roll(u, 2, 0)))
    w = w_ref[...]
    y = w[0:1, :] * u2 + w[1:2, :] * u1 + w[2:3, :] * u
    o_ref[0] = (gb_ref[0].astype(F32) * y).astype(o_ref.dtype)


def short_conv(ze, conv_w, ts_pref=512, halo=16):
    b, s, _ = ze.shape
    ts = _pick(s, ts_pref)
    cw = 4 * LANES
    nc = CONV_DIM // cw

    def cur(col):
        return pl.BlockSpec((1, ts, cw), lambda bi, i, c: (bi, i, col * LANES // cw + c))

    def prev(col):
        return pl.BlockSpec((1, halo, cw), lambda bi, i, c: (bi, jnp.maximum(i * (ts // halo) - 1, 0), col * LANES // cw + c))

    return pl.pallas_call(
        _conv_kernel,
        grid=(b, s // ts, nc),
        in_specs=[cur(ZE_GB), cur(ZE_GC), cur(ZE_HC), prev(ZE_GC), prev(ZE_HC),
                  pl.BlockSpec((CONV_K, cw), lambda bi, i, c: (0, c))],
        out_specs=pl.BlockSpec((1, ts, cw), lambda bi, i, c: (bi, i, c)),
        out_shape=jax.ShapeDtypeStruct((b, s, CONV_DIM), BF16),
        compiler_params=_cparams("parallel", "parallel", "parallel"),
        name="short_conv",
    )(ze, ze, ze, ze, ze, conv_w.astype(F32))


def _swa_kernel(q_ref, kp_ref, kc_ref, vp_ref, vc_ref, sink_ref, tb_ref, o_ref):
    i = pl.program_id(1)
    tq = QBLK
    half = LANES // 2
    nt = (((1,), (1,)), ((), ()))
    q = q_ref[0]
    k_all = jnp.concatenate([kp_ref[0], kc_ref[0]], axis=0).astype(F32) * (C_DH ** -0.5)
    v_all = jnp.concatenate([vp_ref[0], vc_ref[0]], axis=0).astype(F32)
    in_lo = lax.broadcasted_iota(jnp.int32, (2 * tq, LANES), 1) < half
    row_lo = lax.broadcasted_iota(jnp.int32, (LANES, tq), 0) < half
    ones = jnp.ones((LANES, 2 * tq), BF16)
    prev_tile = jnp.where(i > 0, 1, 2)
    pairs_per_kv = C_GRP // 2
    blocks = []
    for kv in range(C_KV):
        lanes = slice((kv // 2) * LANES, (kv // 2 + 1) * LANES)
        own = in_lo if kv % 2 == 0 else ~in_lo
        k_own = jnp.where(own, k_all[:, lanes], 0.0)
        v_own = jnp.where(own, v_all[:, lanes], 0.0)
        k_par = [k_own, pltpu.roll(k_own, half, 1)]
        v_par = [v_own, pltpu.roll(v_own, half, 1)]
        if kv % 2:
            k_par.reverse()
            v_par.reverse()
        q2 = jnp.concatenate([q[:, (kv * pairs_per_kv + m) * LANES:(kv * pairs_per_kv + m + 1) * LANES]
                              for m in range(pairs_per_kv)], axis=0)
        res_t = []
        for par in range(2):
            hs = [kv * C_GRP + 2 * m + par for m in range(pairs_per_kv)]
            s_t = lax.dot_general(k_par[par].astype(BF16), q2, nt, preferred_element_type=F32)
            bias_t = jnp.concatenate([jnp.concatenate([tb_ref[h, prev_tile], tb_ref[h, 0]], axis=0) for h in hs], axis=1)
            sink = jnp.concatenate([jnp.full((1, tq), sink_ref[h], F32) for h in hs], axis=1)
            s_t = s_t + bias_t
            m_col = jnp.maximum(jnp.max(s_t, axis=0, keepdims=True), sink)
            e_t = jnp.exp(s_t - m_col).astype(BF16)
            vo = jnp.concatenate([v_par[par].T.astype(BF16), ones], axis=0)
            nd = jnp.dot(vo, e_t, preferred_element_type=F32)
            res_t.append(nd[:LANES] / (nd[LANES:] + jnp.exp(sink - m_col)))
        both = jnp.where(jnp.concatenate([row_lo] * pairs_per_kv, axis=1), res_t[0], res_t[1])
        blocks.extend(both[:, m * tq:(m + 1) * tq].T for m in range(pairs_per_kv))
    o_ref[0] = jnp.concatenate(blocks, axis=-1).astype(o_ref.dtype)


def swa_bucket_idx():
    a = np.arange(QBLK)[:, None]
    c = np.arange(QBLK)[None, :]
    own = jnp.where(a - c >= 0, t5_bucket(a - c), -1).T
    before = jnp.where(QBLK + a - c < WIN_C, t5_bucket(QBLK + a - c), -1).T
    return jnp.stack([own, before, jnp.full((QBLK, QBLK), -1, jnp.int32)]).astype(jnp.int32)


def swa_attention(zo, sinks, tb):
    b, s, _ = zo.shape
    kcol = C_Q // C_KVW
    prev = lambda bi, i: jnp.maximum(i - 1, 0)
    return pl.pallas_call(
        _swa_kernel,
        grid=(b, s // QBLK),
        in_specs=[
            pl.BlockSpec((1, QBLK, C_Q), lambda bi, i: (bi, i, 0)),
            pl.BlockSpec((1, QBLK, C_KVW), lambda bi, i: (bi, prev(bi, i), kcol)),
            pl.BlockSpec((1, QBLK, C_KVW), lambda bi, i: (bi, i, kcol)),
            pl.BlockSpec((1, QBLK, C_KVW), lambda bi, i: (bi, prev(bi, i), kcol + 1)),
            pl.BlockSpec((1, QBLK, C_KVW), lambda bi, i: (bi, i, kcol + 1)),
            pl.BlockSpec(memory_space=pltpu.SMEM),
            pl.BlockSpec((C_HEADS, 3, QBLK, QBLK), lambda bi, i: (0, 0, 0, 0)),
        ],
        out_specs=pl.BlockSpec((1, QBLK, C_Q), lambda bi, i: (bi, i, 0)),
        out_shape=jax.ShapeDtypeStruct((b, s, C_Q), BF16),
        compiler_params=_cparams("parallel", "parallel"),
        name="swa_attention",
    )(zo, zo, zo, zo, zo, sinks.astype(F32), tb)


def _router_kernel(h_ref, g_ref, wr_ref, xn_ref, idx_ref, wt_ref, *, n_experts):
    x = h_ref[...]
    xn = x * lax.rsqrt(jnp.mean(x * x, axis=-1, keepdims=True) + EPS) * g_ref[...]
    packed = _pack_bf16_pairs(xn)
    n_tiles = packed.shape[1] // LANES
    for s in range(n_tiles):
        xn_ref[pl.ds(s, x.shape[0], stride=n_tiles), :] = packed[:, s * LANES:(s + 1) * LANES]
    logits = jnp.dot(xn, wr_ref[...], preferred_element_type=F32, precision=lax.Precision.HIGHEST)
    lane = lax.broadcasted_iota(jnp.int32, logits.shape, 1)
    lg = jnp.where(lane < n_experts, logits, NEG)
    m1 = jnp.max(lg, axis=-1, keepdims=True)
    i1 = jnp.min(jnp.where(lg == m1, lane, LANES), axis=-1, keepdims=True)
    lg2 = jnp.where(lane == i1, NEG, lg)
    m2 = jnp.max(lg2, axis=-1, keepdims=True)
    i2 = jnp.min(jnp.where(lg2 == m2, lane, LANES), axis=-1, keepdims=True)
    e2 = jnp.exp(m2 - m1)
    idx_ref[...] = jnp.where(lane == 0, i1, jnp.where(lane == 1, i2, 0))
    wt_ref[...] = jnp.where(lane == 0, 1.0 / (1.0 + e2), jnp.where(lane == 1, e2 / (1.0 + e2), 0.0))


def router(h, g, w_router):
    t, d = h.shape
    e = w_router.shape[1]
    tm = _pick(t, 512)
    wr = jnp.zeros((d, LANES), F32).at[:, :e].set(w_router.astype(F32))
    row = lambda i: (i, 0)
    return pl.pallas_call(
        functools.partial(_router_kernel, n_experts=e),
        grid=(t // tm,),
        in_specs=[pl.BlockSpec((tm, d), row), pl.BlockSpec((1, d), lambda i: (0, 0)),
                  pl.BlockSpec((d, LANES), lambda i: (0, 0))],
        out_specs=[pl.BlockSpec((tm * (d // 2 // LANES), LANES), row), pl.BlockSpec((tm, LANES), row),
                   pl.BlockSpec((tm, LANES), row)],
        out_shape=[jax.ShapeDtypeStruct((t * (d // 2 // LANES), LANES), jnp.uint32), jax.ShapeDtypeStruct((t, LANES), jnp.int32),
                   jax.ShapeDtypeStruct((t, LANES), F32)],
        compiler_params=_cparams("parallel"),
        name="moe_router",
    )(h, g.reshape(1, d).astype(F32), wr)


def _row_copy(src_hbm, dst_ref, sem, src_row, dst_row):
    return pltpu.make_async_copy(src_hbm.at[pl.ds(src_row, 1), :], dst_ref.at[pl.ds(dst_row, 1), :], sem)


def _token_copy(src_hbm, dst_ref, sem, src_tok, dst_tok, n_tiles):
    return pltpu.make_async_copy(src_hbm.at[pl.ds(pl.multiple_of(src_tok * n_tiles, n_tiles), n_tiles), :],
                                 dst_ref.at[pl.ds(pl.multiple_of(dst_tok * n_tiles, n_tiles), n_tiles), :], sem)


def _gather_kernel(tok_ref, x_hbm, o_ref, buf_ref, sem):
    rows = o_ref.shape[0]
    n_tiles = buf_ref.shape[0] // rows

    def start(r2, c):
        for par in range(2):
            r = 2 * r2 + par
            _token_copy(x_hbm, buf_ref, sem, tok_ref[0, 0, r], r, n_tiles).start(priority=par)
        return c

    def wait(r, c):
        _token_copy(x_hbm, buf_ref, sem, 0, r, n_tiles).wait()
        return c

    lax.fori_loop(0, rows // 2, start, 0)
    lax.fori_loop(0, rows, wait, 0)
    packed = jnp.concatenate([buf_ref[pl.ds(s, rows, stride=n_tiles), :] for s in range(n_tiles)], axis=1)
    o_ref[...] = _unpack_bf16_pairs(packed).astype(o_ref.dtype)


def gather_rows(x, row_tok, tm, d, out_dtype):
    n_rows = row_tok.shape[0]
    n_tiles = d // 2 // LANES
    return pl.pallas_call(
        _gather_kernel,
        grid=(n_rows // tm,),
        in_specs=[pl.BlockSpec((1, 1, tm), lambda i: (i, 0, 0), memory_space=pltpu.SMEM),
                  pl.BlockSpec(memory_space=pl.ANY)],
        out_specs=pl.BlockSpec((tm, d), lambda i: (i, 0)),
        out_shape=jax.ShapeDtypeStruct((n_rows, d), out_dtype),
        scratch_shapes=[pltpu.VMEM((tm * n_tiles, LANES), x.dtype), pltpu.SemaphoreType.DMA(())],
        compiler_params=_cparams("arbitrary"),
        name="moe_gather",
    )(row_tok.reshape(n_rows // tm, 1, tm), x)


def _combine_kernel(p0_ref, p1_ref, y_hbm, wt_ref, h_ref, g_ref, o_ref, on_ref, a_ref, b_ref, sem, *, tile_words):
    rows = o_ref.shape[0]

    def start(r, c):
        _row_copy(y_hbm, a_ref, sem.at[0], p0_ref[0, 0, r], r).start(priority=0)
        _row_copy(y_hbm, b_ref, sem.at[1], p1_ref[0, 0, r], r).start(priority=1)
        return c

    def wait(r, c):
        _row_copy(y_hbm, a_ref, sem.at[0], 0, r).wait()
        _row_copy(y_hbm, b_ref, sem.at[1], 0, r).wait()
        return c

    lax.fori_loop(0, rows, start, 0)
    lax.fori_loop(0, rows, wait, 0)
    wt = wt_ref[...]

    def unpack(ref):
        return jnp.concatenate([_unpack_bf16_pairs(ref[:, c:c + tile_words])
                                for c in range(0, ref.shape[1], tile_words)], axis=1)

    out = h_ref[...] + wt[:, 0:1] * unpack(a_ref) + wt[:, 1:2] * unpack(b_ref)
    o_ref[...] = out
    normed = out * lax.rsqrt(jnp.mean(out * out, axis=-1, keepdims=True) + EPS)
    on_ref[...] = (normed * g_ref[...]).astype(on_ref.dtype)


def combine_rows(y_rows, pos0, pos1, wt, h, g_next, tm, tile_words):
    t, d = h.shape
    idx_spec = pl.BlockSpec((1, 1, tm), lambda i: (i, 0, 0), memory_space=pltpu.SMEM)
    return pl.pallas_call(
        functools.partial(_combine_kernel, tile_words=tile_words),
        grid=(t // tm,),
        in_specs=[idx_spec, idx_spec, pl.BlockSpec(memory_space=pl.ANY),
                  pl.BlockSpec((tm, LANES), lambda i: (i, 0)), pl.BlockSpec((tm, d), lambda i: (i, 0)),
                  pl.BlockSpec((1, d), lambda i: (0, 0))],
        out_specs=[pl.BlockSpec((tm, d), lambda i: (i, 0)), pl.BlockSpec((tm, d), lambda i: (i, 0))],
        out_shape=[jax.ShapeDtypeStruct((t, d), F32), jax.ShapeDtypeStruct((t, d), BF16)],
        scratch_shapes=[pltpu.VMEM((tm, d // 2), jnp.uint32), pltpu.VMEM((tm, d // 2), jnp.uint32),
                        pltpu.SemaphoreType.DMA((2,))],
        compiler_params=_cparams("arbitrary"),
        name="moe_combine",
    )(pos0.reshape(t // tm, 1, tm), pos1.reshape(t // tm, 1, tm), y_rows, wt, h, g_next.reshape(1, d).astype(F32))


def moe_layer(h, g, g_next, w_router, wg, wu, wd, layer, tm_pref=512):
    t, d = h.shape
    n_exp = w_router.shape[1]
    tm = _pick(t, tm_pref)
    xn, idx, wt = router(h, g, w_router)
    e_flat = idx[:, :TOP_K].reshape(-1)
    onehot = (e_flat[:, None] == jnp.arange(n_exp)[None, :]).astype(jnp.int32)
    csum = jnp.cumsum(onehot, axis=0)
    rank = jnp.sum((csum - onehot) * onehot, axis=1)
    counts = csum[-1]
    padded = (counts + tm - 1) // tm * tm
    pad_end = jnp.cumsum(padded)
    dest = (pad_end - padded)[e_flat] + rank
    n_rows = t * TOP_K + n_exp * tm
    row_tok = jnp.zeros((n_rows,), jnp.int32).at[dest].set(jnp.arange(t * TOP_K, dtype=jnp.int32) // TOP_K)
    n_blk = n_rows // tm
    blk = jnp.arange(n_blk)
    n_used = (pad_end[-1:] // tm).astype(jnp.int32)
    blk_expert = jnp.minimum(jnp.searchsorted(pad_end, blk * tm, side='right'), n_exp - 1).astype(jnp.int32)
    blk_expert = jnp.where(blk < n_used[0], blk_expert, blk_expert[n_used[0] - 1])
    first = ((blk == 0) | (blk_expert != jnp.roll(blk_expert, 1))).astype(jnp.int32)
    groups = (layer * n_exp + blk_expert, first, n_used)

    xs = gather_rows(xn, row_tok, tm, d, BF16)
    hid = grouped_glu(xs, wg, wu, groups, tm)
    tn_down = _pick(d, 512)
    y_rows = grouped_matmul([hid], wd, groups, F32, tm, tn_down, pack_out=True)
    pos = dest.reshape(t, TOP_K).astype(jnp.int32)
    return combine_rows(y_rows, pos[:, 0], pos[:, 1], wt, h, g_next, tm, tn_down // 2)


def _even_in_proj_weight(w_in):
    n_layers, d, _ = w_in.shape
    g0 = A_Q + 6 * A_KVW
    gates = w_in[:, :, g0:g0 + 3 * A_HEADS].reshape(n_layers, d, 3, A_KV, A_GRP)
    blocks = [w_in[:, :, :g0], w_in[:, :, g0 + 3 * A_HEADS:]]
    for k in range(A_KV):
        gk = gates[:, :, :, k, :].reshape(n_layers, d, 3 * A_GRP)
        blocks.append(jnp.pad(gk, ((0, 0), (0, 0), (0, LANES - 3 * A_GRP))))
    w = jnp.concatenate(blocks, axis=2)
    return jnp.pad(w, ((0, 0), (0, 0), (0, ZE_WIDTH - w.shape[2])))


def kernel(x, p, rel_bias, norm_mix, norm_ffn, norm_ple, norm_final, w_in_e, cmp_pos, cmp_w1, cmp_w2, conv_w,
           w_out_e, w_gate_d, w_up_d, w_down_d, w_qkv_o, sinks, w_out_o, w_router, w_gate_m, w_up_m, w_down_m,
           w_ple, w_ple_gate):
    b, s, d = x.shape
    t = b * s
    depth = norm_mix.shape[0]
    tm = _pick(t, 1024)
    tm_down = _pick(t, 512)

    tw, tc = nsa_bias_tiles(rel_bias, s)
    tb = bias_table(swa_bucket_idx(), rel_bias, A_HEADS, C_HEADS)

    w_in_relaid = _even_in_proj_weight(w_in_e)
    w_gate_m = w_gate_m.reshape((-1,) + w_gate_m.shape[2:])
    w_up_m = w_up_m.reshape((-1,) + w_up_m.shape[2:])
    w_down_m = w_down_m.reshape((-1,) + w_down_m.shape[2:])
    p = p.reshape(depth, t, -1)

    h = x.reshape(t, d).astype(F32)
    hn = rmsnorm(h, norm_mix[0], BF16)
    for i in range(depth):
        j = i // 2
        layer = _dense_groups(t // tm, j)
        half_blocks = _dense_groups(t // tm_down, j)
        if i % 2 == 0:
            ze = grouped_matmul([hn], w_in_relaid, layer, BF16, tm, 2048).reshape(b, s, ZE_WIDTH)
            kvcm = compress(ze, cmp_pos[j], cmp_w1[j], cmp_w2[j])
            att = nsa_attention(ze, kvcm, tw, tc).reshape(t, A_Q)
            cnv = short_conv(ze, conv_w[j]).reshape(t, CONV_DIM)
            h, hn = grouped_matmul([att, cnv], w_out_e, half_blocks, F32, tm_down, d, res=h, norm_gain=norm_ffn[i])
            hid = grouped_glu(hn, w_gate_d, w_up_d, half_blocks, tm_down)
            h = grouped_matmul([hid], w_down_d, half_blocks, F32, tm_down, 512, res=h)
            hn = rmsnorm(h, norm_ple[i], BF16)
        else:
            zo = grouped_matmul([hn], w_qkv_o, layer, BF16, tm, 1280).reshape(b, s, -1)
            att = swa_attention(zo, sinks[j], tb).reshape(t, C_Q)
            h = grouped_matmul([att], w_out_o, half_blocks, F32, tm_down, d, res=h)
            h, hn = moe_layer(h, norm_ffn[i], norm_ple[i], w_router[j], w_gate_m, w_up_m, w_down_m, j)
        last = i + 1 == depth
        h, hn = ple(hn, p, w_ple_gate, w_ple, i, h, norm_final if last else norm_mix[i + 1], x.dtype if last else BF16)
    return hn.reshape(b, s, d)
```

```python
import functools
import math

import numpy as np
import jax
import jax.numpy as jnp
from jax import lax
from jax.experimental import pallas as pl
from jax.experimental.pallas import tpu as pltpu

F32 = jnp.float32
BF16 = jnp.bfloat16

LANES = 128
VMEM_LIMIT_BYTES = 56 * 1024 * 1024
VMEM_LIMIT_STAGED_BYTES = 60 * 1024 * 1024

EPS = 1e-6
A_HEADS = 8
A_KV = 2
A_GRP = A_HEADS // A_KV
A_DH = 128
A_Q = A_HEADS * A_DH
A_KVW = A_KV * A_DH
CMP_LEN = 32
CMP_STRIDE = 16
SLC_BLK = 64
SLC_TOPK = 16
WIN_A = 512
CONV_DIM = 1024
CONV_K = 3
C_HEADS = 32
C_KV = 4
C_GRP = C_HEADS // C_KV
C_DH = 64
C_Q = C_HEADS * C_DH
C_KVW = C_KV * C_DH
WIN_C = 128
N_BUCKETS = 32
MAX_DIST = 1024
TOP_K = 2

QBLK = 128
FAR_REL = 8
NEG = -1e30
BIG = 1e30

ZE_Q = 0
ZE_KC, ZE_VC, ZE_KS, ZE_VS, ZE_KW, ZE_VW = 8, 10, 12, 14, 16, 18
ZE_GB, ZE_GC, ZE_HC = 20, 28, 36
ZE_GATE = 44
ZE_WIDTH = 48 * LANES


def _cparams(*sem, vmem_limit_bytes=VMEM_LIMIT_BYTES):
    return pltpu.CompilerParams(dimension_semantics=sem, vmem_limit_bytes=vmem_limit_bytes)


def _sigmoid(v):
    return 1.0 / (1.0 + jnp.exp(-v))


def _pick(n, pref):
    t = min(n, pref)
    while n % t:
        t -= LANES if t > LANES else 8
    return t


def _pack_bf16_pairs(x):
    half = x.shape[1] // 2
    lo = lax.bitcast_convert_type(x[:, :half].astype(BF16).astype(F32), jnp.uint32)
    hi = lax.bitcast_convert_type(x[:, half:].astype(BF16).astype(F32), jnp.uint32)
    return lax.shift_right_logical(lo, jnp.uint32(16)) | hi


def _unpack_bf16_pairs(w):
    lo = lax.bitcast_convert_type(lax.shift_left(w, jnp.uint32(16)), F32)
    hi = lax.bitcast_convert_type(w & jnp.uint32(0xFFFF0000), F32)
    return jnp.concatenate([lo, hi], axis=1)


def _rmsnorm_kernel(x_ref, g_ref, o_ref):
    x = x_ref[...]
    y = x * lax.rsqrt(jnp.mean(x * x, axis=-1, keepdims=True) + EPS)
    o_ref[...] = (y * g_ref[...]).astype(o_ref.dtype)


def rmsnorm(x, g, out_dtype):
    t, d = x.shape
    tm = _pick(t, 512)
    return pl.pallas_call(
        _rmsnorm_kernel,
        grid=(t // tm,),
        in_specs=[pl.BlockSpec((tm, d), lambda i: (i, 0)), pl.BlockSpec((1, d), lambda i: (0, 0))],
        out_specs=pl.BlockSpec((tm, d), lambda i: (i, 0)),
        out_shape=jax.ShapeDtypeStruct((t, d), out_dtype),
        compiler_params=_cparams("parallel"),
        name="rmsnorm",
    )(x, g.reshape(1, d).astype(F32))


def _dense_groups(n_blk, layer):
    grp = jnp.full((n_blk,), layer, jnp.int32)
    first = (jnp.arange(n_blk) == 0).astype(jnp.int32)
    return grp, first, jnp.full((1,), n_blk, jnp.int32)


def _next_group(groups):
    grp, first, n_used = groups
    n_blk = grp.shape[0]
    blk = jnp.arange(n_blk)
    starts = jnp.where((first == 1) & (blk < n_used[0]), blk, n_blk)
    after = jnp.concatenate([jnp.flip(lax.cummin(jnp.flip(starts)))[1:], jnp.full((1,), n_blk, starts.dtype)])
    return jnp.where(after < n_blk, grp[jnp.minimum(after, n_blk - 1)], -1).astype(jnp.int32)


def _stage_weights(grp_ref, first_ref, nxt_ref, w_hbms, stage_ref, wb_ref, sem, tn):
    j = pl.program_id(0)
    i = pl.program_id(1)

    def tile_copies(grp, col_tile):
        col = pl.multiple_of(col_tile * tn, LANES)
        return [pltpu.make_async_copy(w.at[grp, :, pl.ds(col, tn)], stage_ref.at[k], sem.at[k])
                for k, w in enumerate(w_hbms)]

    @pl.when(first_ref[i] == 1)
    def _():
        @pl.when((i == 0) & (j == 0))
        def _():
            for cp in tile_copies(grp_ref[0], 0):
                cp.start()

        for cp in tile_copies(grp_ref[i], j):
            cp.wait()
        for k in range(len(w_hbms)):
            wb_ref[k] = stage_ref[k].astype(BF16)
        nxt = nxt_ref[i]

        @pl.when(nxt >= 0)
        def _():
            for cp in tile_copies(nxt, j):
                cp.start()

        @pl.when((nxt < 0) & (j + 1 < pl.num_programs(0)))
        def _():
            for cp in tile_copies(grp_ref[0], j + 1):
                cp.start()


def _gmm_kernel(grp_ref, first_ref, nu_ref, nxt_ref, *refs, k_sizes, has_res, has_norm, pack_out):
    n_x = len(k_sizes)
    w_hbm = refs[n_x]
    stage_ref, wb_ref, sem = refs[-3:]
    outs = refs[n_x + 1 + has_res + has_norm:-3]
    o_ref = outs[0]
    i = pl.program_id(1)
    _stage_weights(grp_ref, first_ref, nxt_ref, [w_hbm], stage_ref, wb_ref, sem, wb_ref.shape[2])

    @pl.when(i < nu_ref[0])
    def _():
        acc = None
        off = 0
        for x_ref, k in zip(refs[:n_x], k_sizes):
            part = jnp.dot(x_ref[...], wb_ref[0, off:off + k, :], preferred_element_type=F32)
            acc = part if acc is None else acc + part
            off += k
        if has_res:
            acc = refs[n_x + 1][...] + acc
        o_ref[...] = _pack_bf16_pairs(acc) if pack_out else acc.astype(o_ref.dtype)
        if has_norm:
            normed = acc * lax.rsqrt(jnp.mean(acc * acc, axis=-1, keepdims=True) + EPS)
            outs[1][...] = (normed * refs[n_x + 1 + has_res][...]).astype(outs[1].dtype)

    @pl.when(i >= nu_ref[0])
    def _():
        for o in outs:
            o[...] = jnp.zeros(o.shape, o.dtype)


def grouped_matmul(xs, w, groups, out_dtype, tm, tn_pref, res=None, norm_gain=None, pack_out=False):
    t = xs[0].shape[0]
    kw, n = w.shape[1], w.shape[2]
    k_sizes = tuple(x.shape[1] for x in xs)
    tn = _pick(n, tn_pref)
    assert norm_gain is None or tn == n
    idx = lambda j, i, g, f, nu, nx: (i, j)
    in_specs = [pl.BlockSpec((tm, k), lambda j, i, g, f, nu, nx: (i, 0)) for k in k_sizes]
    in_specs.append(pl.BlockSpec(memory_space=pl.ANY))
    args = list(xs) + [w]
    if res is not None:
        in_specs.append(pl.BlockSpec((tm, tn), idx))
        args.append(res)
    if norm_gain is not None:
        in_specs.append(pl.BlockSpec((1, tn), lambda j, i, g, f, nu, nx: (0, 0)))
        args.append(norm_gain.reshape(1, n).astype(F32))
    out_specs = [pl.BlockSpec((tm, tn // 2 if pack_out else tn), idx)]
    out_shape = [jax.ShapeDtypeStruct((t, n // 2), jnp.uint32) if pack_out else jax.ShapeDtypeStruct((t, n), out_dtype)]
    if norm_gain is not None:
        out_specs.append(pl.BlockSpec((tm, tn), idx))
        out_shape.append(jax.ShapeDtypeStruct((t, n), BF16))
    grid_spec = pltpu.PrefetchScalarGridSpec(
        num_scalar_prefetch=4,
        grid=(n // tn, t // tm),
        in_specs=in_specs,
        out_specs=out_specs,
        scratch_shapes=[pltpu.VMEM((1, kw, tn), F32), pltpu.VMEM((1, kw, tn), BF16), pltpu.SemaphoreType.DMA((1,))],
    )
    out = pl.pallas_call(
        functools.partial(_gmm_kernel, k_sizes=k_sizes, has_res=res is not None, has_norm=norm_gain is not None,
                          pack_out=pack_out),
        grid_spec=grid_spec,
        out_shape=out_shape,
        compiler_params=_cparams("arbitrary", "arbitrary", vmem_limit_bytes=VMEM_LIMIT_STAGED_BYTES),
        name="matmul",
    )(*groups, _next_group(groups), *args)
    return out if norm_gain is not None else out[0]


def _glu_kernel(grp_ref, first_ref, nu_ref, nxt_ref, x_ref, wg_hbm, wu_hbm, o_ref, stage_ref, wb_ref, sem):
    i = pl.program_id(1)
    _stage_weights(grp_ref, first_ref, nxt_ref, [wg_hbm, wu_hbm], stage_ref, wb_ref, sem, o_ref.shape[1])

    @pl.when(i < nu_ref[0])
    def _():
        x = x_ref[...]
        g = jnp.dot(x, wb_ref[0], preferred_element_type=F32)
        u = jnp.dot(x, wb_ref[1], preferred_element_type=F32)
        o_ref[...] = (g * _sigmoid(g) * u).astype(o_ref.dtype)

    @pl.when(i >= nu_ref[0])
    def _():
        o_ref[...] = jnp.zeros(o_ref.shape, o_ref.dtype)


def grouped_glu(x, wg, wu, groups, tm, tf_pref=1408):
    t, d = x.shape
    f = wg.shape[2]
    tf = _pick(f, tf_pref)
    grid_spec = pltpu.PrefetchScalarGridSpec(
        num_scalar_prefetch=4,
        grid=(f // tf, t // tm),
        in_specs=[pl.BlockSpec((tm, d), lambda j, i, g, fl, nu, nx: (i, 0)),
                  pl.BlockSpec(memory_space=pl.ANY), pl.BlockSpec(memory_space=pl.ANY)],
        out_specs=pl.BlockSpec((tm, tf), lambda j, i, g, fl, nu, nx: (i, j)),
        scratch_shapes=[pltpu.VMEM((2, d, tf), F32), pltpu.VMEM((2, d, tf), BF16), pltpu.SemaphoreType.DMA((2,))],
    )
    return pl.pallas_call(
        _glu_kernel,
        grid_spec=grid_spec,
        out_shape=jax.ShapeDtypeStruct((t, f), BF16),
        compiler_params=_cparams("arbitrary", "arbitrary", vmem_limit_bytes=VMEM_LIMIT_STAGED_BYTES),
        name="glu",
    )(*groups, _next_group(groups), x, wg, wu)


def _ple_kernel(grp_ref, first_ref, nu_ref, nxt_ref, hn_ref, p_ref, wg_hbm, wp_ref, h_ref, gn_ref, o_ref, on_ref,
                stage_ref, wb_ref, sem, wpb_ref):
    _stage_weights(grp_ref, first_ref, nxt_ref, [wg_hbm], stage_ref, wb_ref, sem, wb_ref.shape[2])

    @pl.when(pl.program_id(1) == 0)
    def _():
        wpb_ref[...] = wp_ref[...].astype(BF16)

    gate = _sigmoid(jnp.dot(hn_ref[...], wb_ref[0], preferred_element_type=F32))
    pe = jnp.dot(p_ref[...].astype(BF16), wpb_ref[...], preferred_element_type=F32)
    out = h_ref[...] + gate * pe
    o_ref[...] = out
    normed = out * lax.rsqrt(jnp.mean(out * out, axis=-1, keepdims=True) + EPS)
    on_ref[...] = (normed * gn_ref[...]).astype(on_ref.dtype)


def ple(hn, p, wg, wp, layer, h, g_next, next_dtype, tm_pref=256):
    t, d = h.shape
    pd = p.shape[2]
    tm = _pick(t, tm_pref)
    row = lambda j, i, g, f, nu, nx: (i, 0)
    grid_spec = pltpu.PrefetchScalarGridSpec(
        num_scalar_prefetch=4,
        grid=(1, t // tm),
        in_specs=[
            pl.BlockSpec((tm, d), row),
            pl.BlockSpec((None, tm, pd), lambda j, i, g, f, nu, nx: (layer, i, 0)),
            pl.BlockSpec(memory_space=pl.ANY),
            pl.BlockSpec((None, pd, d), lambda j, i, g, f, nu, nx: (layer, 0, 0)),
            pl.BlockSpec((tm, d), row),
            pl.BlockSpec((1, d), lambda j, i, g, f, nu, nx: (0, 0)),
        ],
        out_specs=[pl.BlockSpec((tm, d), row), pl.BlockSpec((tm, d), row)],
        scratch_shapes=[pltpu.VMEM((1, d, d), F32), pltpu.VMEM((1, d, d), BF16), pltpu.SemaphoreType.DMA((1,)),
                        pltpu.VMEM((pd, d), BF16)],
    )
    groups = _dense_groups(t // tm, layer)
    return pl.pallas_call(
        _ple_kernel,
        grid_spec=grid_spec,
        out_shape=[jax.ShapeDtypeStruct((t, d), F32), jax.ShapeDtypeStruct((t, d), next_dtype)],
        compiler_params=_cparams("arbitrary", "arbitrary", vmem_limit_bytes=VMEM_LIMIT_STAGED_BYTES),
        name="ple",
    )(*groups, _next_group(groups), hn, p, wg, wp, h, g_next.reshape(1, d).astype(F32))


def t5_bucket(dist):
    n = jnp.maximum(jnp.asarray(dist, jnp.int32), 0)
    exact = N_BUCKETS // 2
    nf = jnp.maximum(n, 1).astype(F32)
    large = exact + (jnp.log(nf / exact) / math.log(MAX_DIST / exact) * (N_BUCKETS - exact)).astype(jnp.int32)
    return jnp.where(n < exact, n, jnp.minimum(large, N_BUCKETS - 1))


def _bias_table_kernel(idx_ref, tab_ref, o_ref, *, head0):
    n_heads, _, rows, _ = o_ref.shape
    chunk = max(8, 256 // n_heads)
    for r0 in range(0, rows, chunk):
        idx = idx_ref[0, r0:r0 + chunk, :]
        vals = [jnp.full(idx.shape, tab_ref[0, head0 + h], F32) for h in range(n_heads)]
        for b in range(1, N_BUCKETS):
            hit = idx == b
            vals = [jnp.where(hit, tab_ref[b, head0 + h], v) for h, v in enumerate(vals)]
        for h, v in enumerate(vals):
            o_ref[h, 0, r0:r0 + chunk, :] = jnp.where(idx < 0, NEG, v)


def bias_table(bucket_idx, rel_bias, head0, n_heads):
    n, r, c = bucket_idx.shape
    return pl.pallas_call(
        functools.partial(_bias_table_kernel, head0=head0),
        grid=(n,),
        in_specs=[
            pl.BlockSpec((1, r, c), lambda i: (i, 0, 0)),
            pl.BlockSpec(memory_space=pltpu.SMEM),
        ],
        out_specs=pl.BlockSpec((n_heads, 1, r, c), lambda i: (0, i, 0, 0)),
        out_shape=jax.ShapeDtypeStruct((n_heads, n, r, c), F32),
        compiler_params=_cparams("parallel"),
        name="bias_table",
    )(bucket_idx, rel_bias.astype(F32))


def tile_bucket_idx(n_rel):
    a = np.arange(QBLK)[None, :, None]
    c = np.arange(QBLK)[None, None, :]
    r = np.arange(n_rel)[:, None, None]
    return t5_bucket(QBLK * r + a - c)


def cmp_bucket_idx(s, n_cmp_pad):
    t = np.arange(s).reshape(s // QBLK, QBLK, 1)
    cend = (np.arange(n_cmp_pad) * CMP_STRIDE + CMP_LEN - 1)[None, None, :]
    return t5_bucket(t - cend)


def _compress_kernel(t_ref, pos_ref, w1_ref, w2_ref, o_ref, tf_ref, *, n_pad):
    s = t_ref.shape[1]
    half = CMP_LEN // 2
    tf_ref[0:s, :] = t_ref[0].astype(F32)
    tf_ref[s:s + half, :] = jnp.zeros((half, A_DH), F32)
    hid = jnp.zeros((n_pad, w1_ref.shape[2]), F32)
    for l in range(CMP_LEN):
        rows = tf_ref[pl.ds(l, n_pad, stride=CMP_STRIDE), :] + pos_ref[0, l:l + 1, :]
        hid = hid + jnp.dot(rows.astype(BF16), w1_ref[0, l * A_DH:(l + 1) * A_DH, :], preferred_element_type=F32)
    act = jax.nn.gelu(hid)
    out = jnp.dot(act.astype(BF16), w2_ref[0], preferred_element_type=F32)
    valid = lax.broadcasted_iota(jnp.int32, out.shape, 0) < n_pad - 1
    o_ref[0, 0, 0] = jnp.where(valid, out, 0.0).astype(o_ref.dtype)


def compress(ze, cmp_pos, cmp_w1, cmp_w2):
    b, s, _ = ze.shape
    n_pad = s // CMP_STRIDE
    hid = cmp_w1.shape[2]
    return pl.pallas_call(
        functools.partial(_compress_kernel, n_pad=n_pad),
        grid=(2, b, A_KV),
        in_specs=[
            pl.BlockSpec((1, s, A_DH), lambda w, bi, k: (bi, 0, ZE_KC + 2 * w + k)),
            pl.BlockSpec((1, CMP_LEN, A_DH), lambda w, bi, k: (w, 0, 0)),
            pl.BlockSpec((1, CMP_LEN * A_DH, hid), lambda w, bi, k: (w, 0, 0)),
            pl.BlockSpec((1, hid, A_DH), lambda w, bi, k: (w, 0, 0)),
        ],
        out_specs=pl.BlockSpec((1, 1, 1, n_pad, A_DH), lambda w, bi, k: (w, bi, k, 0, 0)),
        out_shape=jax.ShapeDtypeStruct((2, b, A_KV, n_pad, A_DH), BF16),
        scratch_shapes=[pltpu.VMEM((s + CMP_LEN // 2, A_DH), F32)],
        compiler_params=_cparams("parallel", "parallel", "parallel"),
        name="nsa_compress",
    )(ze, cmp_pos.astype(F32), cmp_w1.astype(BF16), cmp_w2.astype(BF16))


def _nsa_kernel(q_ref, kcm_ref, vcm_ref, ks_ref, vs_ref, kw_ref, vw_ref, g_ref, tw_ref, tc_ref, o_ref,
                qs_ref, vct_ref, vst_ref, vwt_ref, sel_ref, m_ref, l_ref, acc_ref):
    i = pl.program_id(2)
    tq = QBLK
    hq = A_GRP * tq
    s_len = ks_ref.shape[1]
    n_cmp = kcm_ref.shape[3]
    n_slc = s_len // SLC_BLK
    nt = (((1,), (1,)), ((), ()))
    heads = [slice(g * tq, (g + 1) * tq) for g in range(A_GRP)]

    def per_head(x):
        return jnp.concatenate([x] * A_GRP, axis=1)

    @pl.when(i == 0)
    def _():
        vct_ref[...] = vcm_ref[0, 0, 0].astype(F32).T.astype(BF16)

        def transpose_tile(t, c):
            off = pl.multiple_of(t * tq, tq)
            vst_ref[:, pl.ds(off, tq)] = vs_ref[0, pl.ds(off, tq), :].astype(F32).T.astype(BF16)
            vwt_ref[:, pl.ds(off, tq)] = vw_ref[0, pl.ds(off, tq), :].astype(F32).T.astype(BF16)
            return c

        lax.fori_loop(0, s_len // tq, transpose_tile, 0)

    q = q_ref[0]
    for g in range(A_GRP):
        qs_ref[heads[g], :] = (q[:, g * A_DH:(g + 1) * A_DH].astype(F32) * (A_DH ** -0.5)).astype(BF16)
    t_row = i * tq + lax.broadcasted_iota(jnp.int32, (1, tq), 1)

    n_col = lax.broadcasted_iota(jnp.int32, (n_cmp, hq), 0)
    t_all = i * tq + (lax.broadcasted_iota(jnp.int32, (n_cmp, hq), 1) & (tq - 1))
    mask_c = (t_all >= n_col * CMP_STRIDE + (CMP_LEN - 1)) & (n_col < n_cmp - 1)
    s_c = lax.dot_general(kcm_ref[0, 0, 0], qs_ref[...], nt, preferred_element_type=F32) + tc_ref[0, 0]
    s_c = jnp.where(mask_c, s_c, NEG)
    e_c = jnp.where(mask_c, jnp.exp(s_c - jnp.max(s_c, axis=0, keepdims=True)), 0.0)
    r_c = 1.0 / jnp.maximum(jnp.sum(e_c, axis=0, keepdims=True), 1e-30)
    o_c = jnp.dot(vct_ref[...], e_c.astype(BF16), preferred_element_type=F32) * r_c
    p_c = e_c * r_c
    p_sum = (p_c[:, heads[0]] + p_c[:, heads[1]]) + (p_c[:, heads[2]] + p_c[:, heads[3]])
    ov_j = lax.broadcasted_iota(jnp.int32, (LANES, n_cmp), 0) * SLC_BLK
    ov_n = lax.broadcasted_iota(jnp.int32, (LANES, n_cmp), 1) * CMP_STRIDE
    overlap_t = ((ov_n < ov_j + SLC_BLK) & (ov_n + CMP_LEN > ov_j)).astype(F32)
    imp_t = jnp.dot(overlap_t, p_sum, preferred_element_type=F32, precision=lax.Precision.HIGHEST)

    jb = lax.broadcasted_iota(jnp.int32, (LANES, tq), 0)
    cur = t_row // SLC_BLK
    imp_t = jnp.where(jb * SLC_BLK > t_row, -BIG, imp_t)
    imp_t = jnp.where((jb == 0) | (jb == cur) | (jb == cur - 1), BIG, imp_t)
    sub = 8
    slabs = [imp_t[r * sub:(r + 1) * sub] for r in range(n_slc // sub)]
    jb_slab = lax.broadcasted_iota(jnp.int32, (sub, tq), 0)
    ranks = [jnp.zeros((sub, tq), F32) for _ in slabs]
    for j2 in range(n_slc):
        other = imp_t[j2:j2 + 1, :]
        for r, v in enumerate(slabs):
            if r * sub > j2:
                ahead = other >= v
            elif r * sub + sub - 1 <= j2:
                ahead = other > v
            else:
                ahead = (other > v) | ((other == v) & (jb_slab + r * sub > j2))
            ranks[r] = ranks[r] + jnp.where(ahead, 1.0, 0.0)
    sel_ref[...] = jnp.concatenate([jnp.where(rk < float(min(SLC_TOPK, n_slc)), 1.0, 0.0) for rk in ranks], axis=0)

    n_w = WIN_A // tq + 1
    jw0 = jnp.maximum(i - WIN_A // tq, 0)
    w_start = pl.multiple_of(jw0 * tq, tq)
    dist_w = t_row - (w_start + lax.broadcasted_iota(jnp.int32, (n_w * tq, tq), 0))
    madd_w = jnp.where((dist_w >= 0) & (dist_w < WIN_A), 0.0, NEG)
    s_w = lax.dot_general(kw_ref[0, pl.ds(w_start, n_w * tq), :], qs_ref[...], nt, preferred_element_type=F32)
    s_w = s_w + jnp.concatenate([tw_ref[0, jnp.clip(i - jw0 - cb, 0, FAR_REL)] for cb in range(n_w)], axis=0)
    s_w = s_w + per_head(madd_w)
    e_w = jnp.exp(s_w - jnp.max(s_w, axis=0, keepdims=True))
    r_w = 1.0 / jnp.maximum(jnp.sum(e_w, axis=0, keepdims=True), 1e-30)
    o_w = jnp.dot(vwt_ref[:, pl.ds(w_start, n_w * tq)], e_w.astype(BF16), preferred_element_type=F32) * r_w

    ck = 4 * tq
    blocks_per_chunk = ck // SLC_BLK
    m_ref[...] = jnp.full(m_ref.shape, NEG, F32)
    l_ref[...] = jnp.zeros(l_ref.shape, F32)
    acc_ref[...] = jnp.zeros(acc_ref.shape, F32)
    key_iota = lax.broadcasted_iota(jnp.int32, (ck, tq), 0)

    def chunk(c, carry):
        start = pl.multiple_of(c * ck, ck)
        sel_rows = sel_ref[pl.ds(pl.multiple_of(c * blocks_per_chunk, blocks_per_chunk), blocks_per_chunk), :]
        chosen = jnp.concatenate([jnp.broadcast_to(sel_rows[b:b + 1, :], (SLC_BLK, tq))
                                  for b in range(blocks_per_chunk)], axis=0)
        madd = jnp.where(jnp.where(start + key_iota <= t_row, chosen, 0.0) > 0.5, 0.0, NEG)
        s_s = lax.dot_general(ks_ref[0, pl.ds(start, ck), :], qs_ref[...], nt, preferred_element_type=F32)
        s_s = s_s + jnp.concatenate([tw_ref[0, jnp.clip(i - (ck // tq) * c - cb, 0, FAR_REL)]
                                     for cb in range(ck // tq)], axis=0)
        s_s = s_s + per_head(madd)
        m_old = m_ref[...]
        m_new = jnp.maximum(m_old, jnp.max(s_s, axis=0, keepdims=True))
        p = jnp.exp(s_s - m_new)
        alpha = jnp.exp(m_old - m_new)
        l_ref[...] = alpha * l_ref[...] + jnp.sum(p, axis=0, keepdims=True)
        acc_ref[...] = alpha * acc_ref[...] + jnp.dot(vst_ref[:, pl.ds(start, ck)], p.astype(BF16),
                                                      preferred_element_type=F32)
        m_ref[...] = m_new
        return carry

    lax.fori_loop(0, (i * tq) // ck + 1, chunk, 0)
    o_s = acc_ref[...] / jnp.maximum(l_ref[...], 1e-30)

    gate_t = _sigmoid(g_ref[0].astype(F32)).T
    outs = []
    for g in range(A_GRP):
        o_t = (gate_t[g:g + 1, :] * o_c[:, heads[g]]
               + gate_t[A_GRP + g:A_GRP + g + 1, :] * o_s[:, heads[g]]
               + gate_t[2 * A_GRP + g:2 * A_GRP + g + 1, :] * o_w[:, heads[g]])
        outs.append(o_t.T)
    o_ref[0] = jnp.concatenate(outs, axis=-1).astype(o_ref.dtype)


def nsa_attention(ze, kvcm, tw, tc):
    b, s, _ = ze.shape
    n_cmp = kvcm.shape[3]
    hq = A_GRP * QBLK

    def kv_spec(col):
        return pl.BlockSpec((1, s, A_DH), lambda bi, k, i: (bi, 0, col + k))

    return pl.pallas_call(
        _nsa_kernel,
        grid=(b, A_KV, s // QBLK),
        in_specs=[
            pl.BlockSpec((1, QBLK, A_GRP * A_DH), lambda bi, k, i: (bi, i, k)),
            pl.BlockSpec((1, 1, 1, n_cmp, A_DH), lambda bi, k, i: (0, bi, k, 0, 0)),
            pl.BlockSpec((1, 1, 1, n_cmp, A_DH), lambda bi, k, i: (1, bi, k, 0, 0)),
            kv_spec(ZE_KS), kv_spec(ZE_VS), kv_spec(ZE_KW), kv_spec(ZE_VW),
            pl.BlockSpec((1, QBLK, LANES), lambda bi, k, i: (bi, i, ZE_GATE + k)),
            pl.BlockSpec((1, FAR_REL + 1, QBLK, hq), lambda bi, k, i: (k, 0, 0, 0)),
            pl.BlockSpec((1, 1, n_cmp, hq), lambda bi, k, i: (k, i, 0, 0)),
        ],
        out_specs=pl.BlockSpec((1, QBLK, A_GRP * A_DH), lambda bi, k, i: (bi, i, k)),
        out_shape=jax.ShapeDtypeStruct((b, s, A_Q), BF16),
        scratch_shapes=[pltpu.VMEM((hq, A_DH), BF16), pltpu.VMEM((A_DH, n_cmp), BF16),
                        pltpu.VMEM((A_DH, s), BF16), pltpu.VMEM((A_DH, s), BF16),
                        pltpu.VMEM((s // SLC_BLK, QBLK), F32),
                        pltpu.VMEM((1, hq), F32), pltpu.VMEM((1, hq), F32), pltpu.VMEM((A_DH, hq), F32)],
        compiler_params=_cparams("parallel", "parallel", "arbitrary"),
        name="nsa_attention",
    )(ze, kvcm, kvcm, ze, ze, ze, ze, ze, tw, tc)


def nsa_bias_tiles(rel_bias, s):
    n_cmp = s // CMP_STRIDE
    tw = bias_table(jnp.swapaxes(tile_bucket_idx(FAR_REL + 1), 1, 2), rel_bias, 0, A_HEADS)
    tw = tw.reshape(A_KV, A_GRP, FAR_REL + 1, QBLK, QBLK).transpose(0, 2, 3, 1, 4)
    tc = bias_table(jnp.swapaxes(cmp_bucket_idx(s, n_cmp), 1, 2), rel_bias, 0, A_HEADS)
    tc = tc.reshape(A_KV, A_GRP, s // QBLK, n_cmp, QBLK).transpose(0, 2, 3, 1, 4)
    return (tw.reshape(A_KV, FAR_REL + 1, QBLK, A_GRP * QBLK), tc.reshape(A_KV, s // QBLK, n_cmp, A_GRP * QBLK))


def _conv_kernel(gb_ref, gc_ref, hc_ref, gcp_ref, hcp_ref, w_ref, o_ref):
    i = pl.program_id(1)
    u = gc_ref[0].astype(F32) * hc_ref[0].astype(F32)
    prev = gcp_ref[0].astype(F32) * hcp_ref[0].astype(F32)
    prev = jnp.where(i > 0, prev, 0.0)
    n_prev = prev.shape[0]
    p1 = prev[n_prev - 1:n_prev, :]
    p2 = prev[n_prev - 2:n_prev - 1, :]
    row = lax.broadcasted_iota(jnp.int32, u.shape, 0)
    u1 = jnp.where(row == 0, p1, pltpu.roll(u, 1, 0))
    u2 = jnp.where(row == 0, p2, jnp.where(row == 1, p1, pltpu.roll(u, 2, 0)))
    w = w_ref[...]
    y = w[0:1, :] * u2 + w[1:2, :] * u1 + w[2:3, :] * u
    o_ref[0] = (gb_ref[0].astype(F32) * y).astype(o_ref.dtype)


def short_conv(ze, conv_w, ts_pref=512, halo=16):
    b, s, _ = ze.shape
    ts = _pick(s, ts_pref)
    cw = 4 * LANES
    nc = CONV_DIM // cw

    def cur(col):
        return pl.BlockSpec((1, ts, cw), lambda bi, i, c: (bi, i, col * LANES // cw + c))

    def prev(col):
        return pl.BlockSpec((1, halo, cw), lambda bi, i, c: (bi, jnp.maximum(i * (ts // halo) - 1, 0), col * LANES // cw + c))

    return pl.pallas_call(
        _conv_kernel,
        grid=(b, s // ts, nc),
        in_specs=[cur(ZE_GB), cur(ZE_GC), cur(ZE_HC), prev(ZE_GC), prev(ZE_HC),
                  pl.BlockSpec((CONV_K, cw), lambda bi, i, c: (0, c))],
        out_specs=pl.BlockSpec((1, ts, cw), lambda bi, i, c: (bi, i, c)),
        out_shape=jax.ShapeDtypeStruct((b, s, CONV_DIM), BF16),
        compiler_params=_cparams("parallel", "parallel", "parallel"),
        name="short_conv",
    )(ze, ze, ze, ze, ze, conv_w.astype(F32))


def _swa_kernel(q_ref, kp_ref, kc_ref, vp_ref, vc_ref, sink_ref, tb_ref, o_ref):
    i = pl.program_id(1)
    tq = QBLK
    half = LANES // 2
    nt = (((1,), (1,)), ((), ()))
    q = q_ref[0]
    k_all = jnp.concatenate([kp_ref[0], kc_ref[0]], axis=0).astype(F32) * (C_DH ** -0.5)
    v_all = jnp.concatenate([vp_ref[0], vc_ref[0]], axis=0).astype(F32)
    in_lo = lax.broadcasted_iota(jnp.int32, (2 * tq, LANES), 1) < half
    row_lo = lax.broadcasted_iota(jnp.int32, (LANES, tq), 0) < half
    ones = jnp.ones((LANES, 2 * tq), BF16)
    prev_tile = jnp.where(i > 0, 1, 2)
    pairs_per_kv = C_GRP // 2
    blocks = []
    for kv in range(C_KV):
        lanes = slice((kv // 2) * LANES, (kv // 2 + 1) * LANES)
        own = in_lo if kv % 2 == 0 else ~in_lo
        k_own = jnp.where(own, k_all[:, lanes], 0.0)
        v_own = jnp.where(own, v_all[:, lanes], 0.0)
        k_par = [k_own, pltpu.roll(k_own, half, 1)]
        v_par = [v_own, pltpu.roll(v_own, half, 1)]
        if kv % 2:
            k_par.reverse()
            v_par.reverse()
        q2 = jnp.concatenate([q[:, (kv * pairs_per_kv + m) * LANES:(kv * pairs_per_kv + m + 1) * LANES]
                              for m in range(pairs_per_kv)], axis=0)
        res_t = []
        for par in range(2):
            hs = [kv * C_GRP + 2 * m + par for m in range(pairs_per_kv)]
            s_t = lax.dot_general(k_par[par].astype(BF16), q2, nt, preferred_element_type=F32)
            bias_t = jnp.concatenate([jnp.concatenate([tb_ref[h, prev_tile], tb_ref[h, 0]], axis=0) for h in hs], axis=1)
            sink = jnp.concatenate([jnp.full((1, tq), sink_ref[h], F32) for h in hs], axis=1)
            s_t = s_t + bias_t
            m_col = jnp.maximum(jnp.max(s_t, axis=0, keepdims=True), sink)
            e_t = jnp.exp(s_t - m_col).astype(BF16)
            vo = jnp.concatenate([v_par[par].T.astype(BF16), ones], axis=0)
            nd = jnp.dot(vo, e_t, preferred_element_type=F32)
            res_t.append(nd[:LANES] / (nd[LANES:] + jnp.exp(sink - m_col)))
        both = jnp.where(jnp.concatenate([row_lo] * pairs_per_kv, axis=1), res_t[0], res_t[1])
        blocks.extend(both[:, m * tq:(m + 1) * tq].T for m in range(pairs_per_kv))
    o_ref[0] = jnp.concatenate(blocks, axis=-1).astype(o_ref.dtype)


def swa_bucket_idx():
    a = np.arange(QBLK)[:, None]
    c = np.arange(QBLK)[None, :]
    own = jnp.where(a - c >= 0, t5_bucket(a - c), -1).T
    before = jnp.where(QBLK + a - c < WIN_C, t5_bucket(QBLK + a - c), -1).T
    return jnp.stack([own, before, jnp.full((QBLK, QBLK), -1, jnp.int32)]).astype(jnp.int32)


def swa_attention(zo, sinks, tb):
    b, s, _ = zo.shape
    kcol = C_Q // C_KVW
    prev = lambda bi, i: jnp.maximum(i - 1, 0)
    return pl.pallas_call(
        _swa_kernel,
        grid=(b, s // QBLK),
        in_specs=[
            pl.BlockSpec((1, QBLK, C_Q), lambda bi, i: (bi, i, 0)),
            pl.BlockSpec((1, QBLK, C_KVW), lambda bi, i: (bi, prev(bi, i), kcol)),
            pl.BlockSpec((1, QBLK, C_KVW), lambda bi, i: (bi, i, kcol)),
            pl.BlockSpec((1, QBLK, C_KVW), lambda bi, i: (bi, prev(bi, i), kcol + 1)),
            pl.BlockSpec((1, QBLK, C_KVW), lambda bi, i: (bi, i, kcol + 1)),
            pl.BlockSpec(memory_space=pltpu.SMEM),
            pl.BlockSpec((C_HEADS, 3, QBLK, QBLK), lambda bi, i: (0, 0, 0, 0)),
        ],
        out_specs=pl.BlockSpec((1, QBLK, C_Q), lambda bi, i: (bi, i, 0)),
        out_shape=jax.ShapeDtypeStruct((b, s, C_Q), BF16),
        compiler_params=_cparams("parallel", "parallel"),
        name="swa_attention",
    )(zo, zo, zo, zo, zo, sinks.astype(F32), tb)


def _router_kernel(h_ref, g_ref, wr_ref, xn_ref, idx_ref, wt_ref, *, n_experts):
    x = h_ref[...]
    xn = x * lax.rsqrt(jnp.mean(x * x, axis=-1, keepdims=True) + EPS) * g_ref[...]
    packed = _pack_bf16_pairs(xn)
    n_tiles = packed.shape[1] // LANES
    for s in range(n_tiles):
        xn_ref[pl.ds(s, x.shape[0], stride=n_tiles), :] = packed[:, s * LANES:(s + 1) * LANES]
    logits = jnp.dot(xn, wr_ref[...], preferred_element_type=F32, precision=lax.Precision.HIGHEST)
    lane = lax.broadcasted_iota(jnp.int32, logits.shape, 1)
    lg = jnp.where(lane < n_experts, logits, NEG)
    m1 = jnp.max(lg, axis=-1, keepdims=True)
    i1 = jnp.min(jnp.where(lg == m1, lane, LANES), axis=-1, keepdims=True)
    lg2 = jnp.where(lane == i1, NEG, lg)
    m2 = jnp.max(lg2, axis=-1, keepdims=True)
    i2 = jnp.min(jnp.where(lg2 == m2, lane, LANES), axis=-1, keepdims=True)
    e2 = jnp.exp(m2 - m1)
    idx_ref[...] = jnp.where(lane == 0, i1, jnp.where(lane == 1, i2, 0))
    wt_ref[...] = jnp.where(lane == 0, 1.0 / (1.0 + e2), jnp.where(lane == 1, e2 / (1.0 + e2), 0.0))


def router(h, g, w_router):
    t, d = h.shape
    e = w_router.shape[1]
    tm = _pick(t, 512)
    wr = jnp.zeros((d, LANES), F32).at[:, :e].set(w_router.astype(F32))
    row = lambda i: (i, 0)
    return pl.pallas_call(
        functools.partial(_router_kernel, n_experts=e),
        grid=(t // tm,),
        in_specs=[pl.BlockSpec((tm, d), row), pl.BlockSpec((1, d), lambda i: (0, 0)),
                  pl.BlockSpec((d, LANES), lambda i: (0, 0))],
        out_specs=[pl.BlockSpec((tm * (d // 2 // LANES), LANES), row), pl.BlockSpec((tm, LANES), row),
                   pl.BlockSpec((tm, LANES), row)],
        out_shape=[jax.ShapeDtypeStruct((t * (d // 2 // LANES), LANES), jnp.uint32), jax.ShapeDtypeStruct((t, LANES), jnp.int32),
                   jax.ShapeDtypeStruct((t, LANES), F32)],
        compiler_params=_cparams("parallel"),
        name="moe_router",
    )(h, g.reshape(1, d).astype(F32), wr)


def _row_copy(src_hbm, dst_ref, sem, src_row, dst_row):
    return pltpu.make_async_copy(src_hbm.at[pl.ds(src_row, 1), :], dst_ref.at[pl.ds(dst_row, 1), :], sem)


def _token_copy(src_hbm, dst_ref, sem, src_tok, dst_tok, n_tiles):
    return pltpu.make_async_copy(src_hbm.at[pl.ds(pl.multiple_of(src_tok * n_tiles, n_tiles), n_tiles), :],
                                 dst_ref.at[pl.ds(pl.multiple_of(dst_tok * n_tiles, n_tiles), n_tiles), :], sem)


def _gather_kernel(tok_ref, nu_ref, x_hbm, o_ref, buf_ref, sem):
    rows = o_ref.shape[0]
    n_tiles = buf_ref.shape[0] // rows
    used = pl.program_id(0) < nu_ref[0]

    def start(r2, c):
        for par in range(2):
            r = 2 * r2 + par
            _token_copy(x_hbm, buf_ref, sem, tok_ref[0, 0, r], r, n_tiles).start(priority=par)
        return c

    def wait(r, c):
        _token_copy(x_hbm, buf_ref, sem, 0, r, n_tiles).wait()
        return c

    @pl.when(used)
    def _():
        lax.fori_loop(0, rows // 2, start, 0)
        lax.fori_loop(0, rows, wait, 0)
        packed = jnp.concatenate([buf_ref[pl.ds(s, rows, stride=n_tiles), :] for s in range(n_tiles)], axis=1)
        o_ref[...] = _unpack_bf16_pairs(packed).astype(o_ref.dtype)

    @pl.when(jnp.logical_not(used))
    def _():
        o_ref[...] = jnp.zeros(o_ref.shape, o_ref.dtype)


def gather_rows(x, row_tok, n_used, tm, d, out_dtype):
    n_rows = row_tok.shape[0]
    n_tiles = d // 2 // LANES
    return pl.pallas_call(
        _gather_kernel,
        grid=(n_rows // tm,),
        in_specs=[pl.BlockSpec((1, 1, tm), lambda i: (i, 0, 0), memory_space=pltpu.SMEM),
                  pl.BlockSpec(memory_space=pltpu.SMEM), pl.BlockSpec(memory_space=pl.ANY)],
        out_specs=pl.BlockSpec((tm, d), lambda i: (i, 0)),
        out_shape=jax.ShapeDtypeStruct((n_rows, d), out_dtype),
        scratch_shapes=[pltpu.VMEM((tm * n_tiles, LANES), x.dtype), pltpu.SemaphoreType.DMA(())],
        compiler_params=_cparams("arbitrary"),
        name="moe_gather",
    )(row_tok.reshape(n_rows // tm, 1, tm), n_used, x)


def _combine_kernel(p0_ref, p1_ref, y_hbm, wt_ref, h_ref, g_ref, o_ref, on_ref, a_ref, b_ref, sem, *, tile_words):
    rows = o_ref.shape[0]

    def start(r, c):
        _row_copy(y_hbm, a_ref, sem.at[0], p0_ref[0, 0, r], r).start(priority=0)
        _row_copy(y_hbm, b_ref, sem.at[1], p1_ref[0, 0, r], r).start(priority=1)
        return c

    def wait(r, c):
        _row_copy(y_hbm, a_ref, sem.at[0], 0, r).wait()
        _row_copy(y_hbm, b_ref, sem.at[1], 0, r).wait()
        return c

    lax.fori_loop(0, rows, start, 0)
    lax.fori_loop(0, rows, wait, 0)
    wt = wt_ref[...]

    def unpack(ref):
        return jnp.concatenate([_unpack_bf16_pairs(ref[:, c:c + tile_words])
                                for c in range(0, ref.shape[1], tile_words)], axis=1)

    out = h_ref[...] + wt[:, 0:1] * unpack(a_ref) + wt[:, 1:2] * unpack(b_ref)
    o_ref[...] = out
    normed = out * lax.rsqrt(jnp.mean(out * out, axis=-1, keepdims=True) + EPS)
    on_ref[...] = (normed * g_ref[...]).astype(on_ref.dtype)


def combine_rows(y_rows, pos0, pos1, wt, h, g_next, tm, tile_words):
    t, d = h.shape
    idx_spec = pl.BlockSpec((1, 1, tm), lambda i: (i, 0, 0), memory_space=pltpu.SMEM)
    return pl.pallas_call(
        functools.partial(_combine_kernel, tile_words=tile_words),
        grid=(t // tm,),
        in_specs=[idx_spec, idx_spec, pl.BlockSpec(memory_space=pl.ANY),
                  pl.BlockSpec((tm, LANES), lambda i: (i, 0)), pl.BlockSpec((tm, d), lambda i: (i, 0)),
                  pl.BlockSpec((1, d), lambda i: (0, 0))],
        out_specs=[pl.BlockSpec((tm, d), lambda i: (i, 0)), pl.BlockSpec((tm, d), lambda i: (i, 0))],
        out_shape=[jax.ShapeDtypeStruct((t, d), F32), jax.ShapeDtypeStruct((t, d), BF16)],
        scratch_shapes=[pltpu.VMEM((tm, d // 2), jnp.uint32), pltpu.VMEM((tm, d // 2), jnp.uint32),
                        pltpu.SemaphoreType.DMA((2,))],
        compiler_params=_cparams("arbitrary"),
        name="moe_combine",
    )(pos0.reshape(t // tm, 1, tm), pos1.reshape(t // tm, 1, tm), y_rows, wt, h, g_next.reshape(1, d).astype(F32))


def moe_layer(h, g, g_next, w_router, wg, wu, wd, layer, tm_pref=512):
    t, d = h.shape
    n_exp = w_router.shape[1]
    tm = _pick(t, tm_pref)
    xn, idx, wt = router(h, g, w_router)
    e_flat = idx[:, :TOP_K].reshape(-1)
    onehot = (e_flat[:, None] == jnp.arange(n_exp)[None, :]).astype(jnp.int32)
    csum = jnp.cumsum(onehot, axis=0)
    rank = jnp.sum((csum - onehot) * onehot, axis=1)
    counts = csum[-1]
    padded = (counts + tm - 1) // tm * tm
    pad_end = jnp.cumsum(padded)
    dest = (pad_end - padded)[e_flat] + rank
    n_rows = t * TOP_K + n_exp * tm
    row_tok = jnp.zeros((n_rows,), jnp.int32).at[dest].set(jnp.arange(t * TOP_K, dtype=jnp.int32) // TOP_K)
    n_blk = n_rows // tm
    blk = jnp.arange(n_blk)
    n_used = (pad_end[-1:] // tm).astype(jnp.int32)
    blk_expert = jnp.minimum(jnp.searchsorted(pad_end, blk * tm, side='right'), n_exp - 1).astype(jnp.int32)
    blk_expert = jnp.where(blk < n_used[0], blk_expert, blk_expert[n_used[0] - 1])
    first = ((blk == 0) | (blk_expert != jnp.roll(blk_expert, 1))).astype(jnp.int32)
    groups = (layer * n_exp + blk_expert, first, n_used)

    xs = gather_rows(xn, row_tok, n_used, tm, d, BF16)
    hid = grouped_glu(xs, wg, wu, groups, tm)
    tn_down = _pick(d, 512)
    y_rows = grouped_matmul([hid], wd, groups, F32, tm, tn_down, pack_out=True)
    pos = dest.reshape(t, TOP_K).astype(jnp.int32)
    return combine_rows(y_rows, pos[:, 0], pos[:, 1], wt, h, g_next, tm, tn_down // 2)


def _even_in_proj_weight(w_in):
    n_layers, d, _ = w_in.shape
    g0 = A_Q + 6 * A_KVW
    gates = w_in[:, :, g0:g0 + 3 * A_HEADS].reshape(n_layers, d, 3, A_KV, A_GRP)
    blocks = [w_in[:, :, :g0], w_in[:, :, g0 + 3 * A_HEADS:]]
    for k in range(A_KV):
        gk = gates[:, :, :, k, :].reshape(n_layers, d, 3 * A_GRP)
        blocks.append(jnp.pad(gk, ((0, 0), (0, 0), (0, LANES - 3 * A_GRP))))
    w = jnp.concatenate(blocks, axis=2)
    return jnp.pad(w, ((0, 0), (0, 0), (0, ZE_WIDTH - w.shape[2])))


def kernel(x, p, rel_bias, norm_mix, norm_ffn, norm_ple, norm_final, w_in_e, cmp_pos, cmp_w1, cmp_w2, conv_w,
           w_out_e, w_gate_d, w_up_d, w_down_d, w_qkv_o, sinks, w_out_o, w_router, w_gate_m, w_up_m, w_down_m,
           w_ple, w_ple_gate):
    b, s, d = x.shape
    t = b * s
    depth = norm_mix.shape[0]
    tm = _pick(t, 1024)
    tm_down = _pick(t, 512)

    tw, tc = nsa_bias_tiles(rel_bias, s)
    tb = bias_table(swa_bucket_idx(), rel_bias, A_HEADS, C_HEADS)

    w_in_relaid = _even_in_proj_weight(w_in_e)
    w_gate_m = w_gate_m.reshape((-1,) + w_gate_m.shape[2:])
    w_up_m = w_up_m.reshape((-1,) + w_up_m.shape[2:])
    w_down_m = w_down_m.reshape((-1,) + w_down_m.shape[2:])
    p = p.reshape(depth, t, -1)

    h = x.reshape(t, d).astype(F32)
    hn = rmsnorm(h, norm_mix[0], BF16)
    for i in range(depth):
        j = i // 2
        layer = _dense_groups(t // tm, j)
        half_blocks = _dense_groups(t // tm_down, j)
        if i % 2 == 0:
            ze = grouped_matmul([hn], w_in_relaid, layer, BF16, tm, 2048).reshape(b, s, ZE_WIDTH)
            kvcm = compress(ze, cmp_pos[j], cmp_w1[j], cmp_w2[j])
            att = nsa_attention(ze, kvcm, tw, tc).reshape(t, A_Q)
            cnv = short_conv(ze, conv_w[j]).reshape(t, CONV_DIM)
            h, hn = grouped_matmul([att, cnv], w_out_e, half_blocks, F32, tm_down, d, res=h, norm_gain=norm_ffn[i])
            hid = grouped_glu(hn, w_gate_d, w_up_d, layer, tm, tf_pref=512)
            h = grouped_matmul([hid], w_down_d, half_blocks, F32, tm_down, 512, res=h)
            hn = rmsnorm(h, norm_ple[i], BF16)
        else:
            zo = grouped_matmul([hn], w_qkv_o, layer, BF16, tm, 1280).reshape(b, s, -1)
            att = swa_attention(zo, sinks[j], tb).reshape(t, C_Q)
            h = grouped_matmul([att], w_out_o, half_blocks, F32, tm_down, d, res=h)
            h, hn = moe_layer(h, norm_ffn[i], norm_ple[i], w_router[j], w_gate_m, w_up_m, w_down_m, j)
        last = i + 1 == depth
        h, hn = ple(hn, p, w_ple_gate, w_ple, i, h, norm_final if last else norm_mix[i + 1], x.dtype if last else BF16)
    return hn.reshape(b, s, d)
```
